```python
import math
import jax, jax.numpy as jnp
from jax import lax
import numpy as np

D_MODEL = 1024
BATCH = 2
SEQ = 8192
DEPTH = 2

N_META = 16
D_RNN = 1280
LRU_BLOCKS = 10
LRU_BLOCK = D_RNN // LRU_BLOCKS
CONV_WIDTH = 4
LRU_C = 8.0
N_HEADS = 8
HEAD_DIM = 128
KV_RANK = 256
IDX_HEADS = 8
IDX_DIM = 64
TOPK_MAX = 256
Q_BLOCK = 128
D_FF = 4 * D_MODEL
DEEPNORM_ALPHA = (2.0 * DEPTH) ** 0.25
DEEPNORM_BETA = (8.0 * DEPTH) ** -0.25
LN_EPS = 1e-5
NEG_INF = -1e30
IN_SPLITS = (D_RNN, D_RNN, N_HEADS * HEAD_DIM, KV_RANK, IDX_HEADS * IDX_DIM, IDX_DIM, IDX_HEADS, D_MODEL, D_MODEL)
D_IN = sum(IN_SPLITS)

kernel_name = "hybrid_rglru_dsa_gated_deepnorm"


def layer_norm(x, g, b):
    xf = x.astype(jnp.float32)
    mu = jnp.mean(xf, axis=-1, keepdims=True)
    var = jnp.mean(jnp.square(xf - mu), axis=-1, keepdims=True)
    return ((xf - mu) * lax.rsqrt(var + LN_EPS) * g.astype(jnp.float32) + b.astype(jnp.float32)).astype(x.dtype)


def unit_layer_norm(x):
    xf = x.astype(jnp.float32)
    mu = jnp.mean(xf, axis=-1, keepdims=True)
    var = jnp.mean(jnp.square(xf - mu), axis=-1, keepdims=True)
    return ((xf - mu) * lax.rsqrt(var + LN_EPS)).astype(x.dtype)


def rms_norm(x, g):
    xf = x.astype(jnp.float32)
    ms = jnp.mean(jnp.square(xf), axis=-1, keepdims=True)
    return (xf * lax.rsqrt(ms + LN_EPS) * g.astype(jnp.float32)).astype(x.dtype)


def split_columns(proj):
    offsets = []
    acc = 0
    for w in IN_SPLITS[:-1]:
        acc += w
        offsets.append(acc)
    return jnp.split(proj, offsets, axis=-1)


def causal_depthwise_conv(x, w, b):
    y = lax.conv_general_dilated(
        x, w[:, None, :], window_strides=(1,), padding=[(CONV_WIDTH - 1, 0)],
        dimension_numbers=('NWC', 'WIO', 'NWC'), feature_group_count=x.shape[-1])
    return y + b


def block_diag_linear(x, w, b):
    B, T, _ = x.shape
    xb = x.reshape(B, T, LRU_BLOCKS, LRU_BLOCK)
    return jnp.einsum('btnc,ncd->btnd', xb, w).reshape(B, T, D_RNN) + b


def rg_lru(x, w_a, b_a, w_x, b_x, lam):
    r = jax.nn.sigmoid(block_diag_linear(x, w_a, b_a)).astype(jnp.float32)
    i = jax.nn.sigmoid(block_diag_linear(x, w_x, b_x)).astype(jnp.float32)
    log_a = -LRU_C * r * jax.nn.softplus(-lam.astype(jnp.float32))
    a = jnp.exp(log_a)
    u = jnp.sqrt(-jnp.expm1(2.0 * log_a)) * (i * x.astype(jnp.float32))

    def combine(left, right):
        a1, b1 = left
        a2, b2 = right
        return a1 * a2, a2 * b1 + b2

    _, h = lax.associative_scan(combine, (a, u), axis=1)
    return h.astype(x.dtype)


def recurrent_branch(x_lru, g_lru, conv_w, conv_b, w_rg_a, b_rg_a, w_rg_x, b_rg_x, lam, w_branch):
    xc = causal_depthwise_conv(x_lru, conv_w, conv_b)
    h = rg_lru(xc, w_rg_a, b_rg_a, w_rg_x, b_rg_x, lam)
    return (h * jax.nn.gelu(g_lru)) @ w_branch


def sparse_attention(q, c_kv, q_idx, k_idx, w_idx, w_uk, w_uv, k_top):
    B, T = q.shape[0], q.shape[1]
    q_abs = jnp.einsum('bthd,hrd->bthr', q, w_uk)
    n_blk = -(-T // Q_BLOCK)
    Tp = n_blk * Q_BLOCK
    pad = Tp - T

    def to_blocks(a):
        a = jnp.pad(a, [(0, 0), (0, pad)] + [(0, 0)] * (a.ndim - 2))
        return jnp.moveaxis(a.reshape((B, n_blk, Q_BLOCK) + a.shape[2:]), 1, 0)

    q_pos = jnp.arange(Tp, dtype=jnp.int32).reshape(n_blk, Q_BLOCK)
    key_pos = jnp.arange(T, dtype=jnp.int32)
    batch_idx = jnp.arange(B, dtype=jnp.int32)[:, None, None]
    scale = HEAD_DIM ** -0.5

    def one_block(args):
        qa, qi, wi, qpos = args
        logits = jnp.einsum('bqhd,bsd->bqhs', qi, k_idx)
        score = jnp.einsum('bqh,bqhs->bqs', wi, jax.nn.relu(logits)).astype(jnp.float32)
        causal = key_pos[None, :] <= qpos[:, None]
        score = jnp.where(causal[None], score, NEG_INF)
        _, idx = lax.top_k(score, k_top)
        valid = idx <= qpos[None, :, None]
        c_sel = c_kv[batch_idx, idx]
        s = jnp.einsum('bqhr,bqkr->bqhk', qa, c_sel).astype(jnp.float32) * scale
        s = jnp.where(valid[:, :, None, :], s, NEG_INF)
        p = jax.nn.softmax(s, axis=-1).astype(c_kv.dtype)
        return jnp.einsum('bqhk,bqkr->bqhr', p, c_sel)

    o_lat = lax.map(one_block, (to_blocks(q_abs), to_blocks(q_idx), to_blocks(w_idx), q_pos))
    o_lat = jnp.moveaxis(o_lat, 0, 1).reshape(B, Tp, N_HEADS, KV_RANK)[:, :T]
    o = jnp.einsum('bthr,hrd->bthd', o_lat, w_uv)
    return o.reshape(B, T, N_HEADS * HEAD_DIM)


def hybrid_layer(x, k_top, w_in, conv_w, conv_b, w_rg_a, b_rg_a, w_rg_x, b_rg_x, lam, kv_norm_g,
                 w_uk, w_uv, w_branch_a, w_branch_b, w_out, ln1_g, ln1_b,
                 w_up, b_up, w_down, b_down, ln2_g, ln2_b):
    B, T, _ = x.shape
    proj = x @ w_in
    x_lru, g_lru, q, c_kv, q_idx, k_idx, w_idx, gate_a, gate_b = split_columns(proj)

    y_a = recurrent_branch(x_lru, g_lru, conv_w, conv_b, w_rg_a, b_rg_a, w_rg_x, b_rg_x, lam, w_branch_a)

    q = q.reshape(B, T, N_HEADS, HEAD_DIM)
    c_kv = rms_norm(c_kv, kv_norm_g)
    q_idx = q_idx.reshape(B, T, IDX_HEADS, IDX_DIM)
    k_idx = unit_layer_norm(k_idx)
    w_idx = w_idx * (IDX_HEADS ** -0.5 * IDX_DIM ** -0.5)
    y_b = sparse_attention(q, c_kv, q_idx, k_idx, w_idx, w_uk, w_uv, k_top) @ w_branch_b

    mixed = jax.nn.sigmoid(gate_a) * y_a + jax.nn.sigmoid(gate_b) * y_b
    x = layer_norm(DEEPNORM_ALPHA * x + mixed @ w_out, ln1_g, ln1_b)

    h = jnp.square(jax.nn.relu(x @ w_up + b_up))
    x = layer_norm(DEEPNORM_ALPHA * x + (h @ w_down + b_down), ln2_g, ln2_b)
    return x


def setup_inputs(seed: int = 0) -> dict:
    key = jax.random.key(seed)
    ks = jax.random.split(key, 32)
    f32 = jnp.float32

    def nrm(k, shape, scale):
        return jax.random.normal(k, shape, f32) * scale

    u = jax.random.uniform(ks[10], (DEPTH, D_RNN), f32, 0.9, 0.999)
    a0 = u ** (1.0 / LRU_C)
    lam = jnp.log(a0) - jnp.log1p(-a0)
    return {
        "x": nrm(ks[0], (BATCH, SEQ, D_MODEL), 1.0),
        "meta_tokens": nrm(ks[1], (N_META, D_MODEL), 1.0),
        "ln_in_g": 1.0 + nrm(ks[2], (D_MODEL,), 0.02),
        "ln_in_b": nrm(ks[3], (D_MODEL,), 0.02),
        "w_in": nrm(ks[4], (DEPTH, D_MODEL, D_IN), D_MODEL ** -0.5),
        "conv_w": nrm(ks[5], (DEPTH, CONV_WIDTH, D_RNN), CONV_WIDTH ** -0.5),
        "conv_b": nrm(ks[6], (DEPTH, D_RNN), 0.02),
        "w_rg_a": nrm(ks[7], (DEPTH, LRU_BLOCKS, LRU_BLOCK, LRU_BLOCK), LRU_BLOCK ** -0.5),
        "b_rg_a": nrm(ks[8], (DEPTH, D_RNN), 0.02),
        "w_rg_x": nrm(ks[9], (DEPTH, LRU_BLOCKS, LRU_BLOCK, LRU_BLOCK), LRU_BLOCK ** -0.5),
        "b_rg_x": nrm(ks[11], (DEPTH, D_RNN), 0.02),
        "lru_lambda": lam,
        "kv_norm_g": 1.0 + nrm(ks[12], (DEPTH, KV_RANK), 0.02),
        "w_uk": nrm(ks[13], (DEPTH, N_HEADS, KV_RANK, HEAD_DIM), KV_RANK ** -0.5),
        "w_uv": nrm(ks[14], (DEPTH, N_HEADS, KV_RANK, HEAD_DIM), KV_RANK ** -0.5),
        "w_branch_a": nrm(ks[15], (DEPTH, D_RNN, D_MODEL), D_RNN ** -0.5),
        "w_branch_b": nrm(ks[16], (DEPTH, N_HEADS * HEAD_DIM, D_MODEL), (N_HEADS * HEAD_DIM) ** -0.5),
        "w_out": nrm(ks[17], (DEPTH, D_MODEL, D_MODEL), D_MODEL ** -0.5 * DEEPNORM_BETA),
        "ln1_g": 1.0 + nrm(ks[18], (DEPTH, D_MODEL), 0.02),
        "ln1_b": nrm(ks[19], (DEPTH, D_MODEL), 0.02),
        "w_up": nrm(ks[20], (DEPTH, D_MODEL, D_FF), D_MODEL ** -0.5),
        "b_up": nrm(ks[21], (DEPTH, D_FF), 0.02),
        "w_down": nrm(ks[22], (DEPTH, D_FF, D_MODEL), D_FF ** -0.5 * DEEPNORM_BETA),
        "b_down": nrm(ks[23], (DEPTH, D_MODEL), 0.02),
        "ln2_g": 1.0 + nrm(ks[24], (DEPTH, D_MODEL), 0.02),
        "ln2_b": nrm(ks[25], (DEPTH, D_MODEL), 0.02),
    }


def reference(x, meta_tokens, ln_in_g, ln_in_b, w_in, conv_w, conv_b, w_rg_a, b_rg_a, w_rg_x, b_rg_x,
              lru_lambda, kv_norm_g, w_uk, w_uv, w_branch_a, w_branch_b, w_out, ln1_g, ln1_b,
              w_up, b_up, w_down, b_down, ln2_g, ln2_b):
    B, S, _ = x.shape
    meta = jnp.broadcast_to(meta_tokens[None].astype(x.dtype), (B, N_META, D_MODEL))
    h = jnp.concatenate([meta, x], axis=1)
    T = S + N_META
    k_top = min(TOPK_MAX, T // 4)
    h = layer_norm(h, ln_in_g, ln_in_b)
    for l in range(DEPTH):
        h = hybrid_layer(h, k_top, w_in[l], conv_w[l], conv_b[l], w_rg_a[l], b_rg_a[l], w_rg_x[l], b_rg_x[l],
                         lru_lambda[l], kv_norm_g[l], w_uk[l], w_uv[l], w_branch_a[l], w_branch_b[l],
                         w_out[l], ln1_g[l], ln1_b[l], w_up[l], b_up[l], w_down[l], b_down[l],
                         ln2_g[l], ln2_b[l])
    return h[:, N_META:]
```

```python
import functools
import math

import jax
import jax.numpy as jnp
from jax import lax
from jax.experimental import pallas as pl
from jax.experimental.pallas import tpu as pltpu

D_MODEL = 1024
N_META = 16
D_RNN = 1280
LRU_BLOCKS = 10
LRU_BLOCK = D_RNN // LRU_BLOCKS
CONV_WIDTH = 4
LRU_C = 8.0
N_HEADS = 8
HEAD_DIM = 128
KV_RANK = 256
IDX_HEADS = 8
IDX_DIM = 64
TOPK_MAX = 256
D_FF = 4 * D_MODEL
LN_EPS = 1e-5
NEG_INF = -1e30

LANES = 128
ROW_TILE = 128
PADF = ROW_TILE - N_META
KEY_CHUNK = 512
VMEM_LIMIT = 56 * 1024 * 1024
INT_MIN = -(2 ** 31)

P_LRU, P_G, P_Q, P_CKV, P_QIDX, P_KIDX, P_WIDX, P_GA, P_GB = (
    D_RNN, D_RNN, N_HEADS * HEAD_DIM, KV_RANK, IDX_HEADS * IDX_DIM, LANES, LANES, D_MODEL, D_MODEL)
P_OFFS = []
_acc = 0
for _w in (P_LRU, P_G, P_Q, P_CKV, P_QIDX, P_KIDX, P_WIDX, P_GA, P_GB):
    P_OFFS.append(_acc)
    _acc += _w
P_TOTAL = _acc


def _cparams(sem):
    return pltpu.CompilerParams(dimension_semantics=sem, vmem_limit_bytes=VMEM_LIMIT)


def _resident(shape):
    nd = len(shape)
    return pl.BlockSpec(shape, lambda *_: (0,) * nd, pipeline_mode=pl.Buffered(1))


def _layer_norm(x, g, b):
    mu = jnp.mean(x, axis=-1, keepdims=True)
    d = x - mu
    var = jnp.mean(d * d, axis=-1, keepdims=True)
    return d * lax.rsqrt(var + LN_EPS) * g + b


def _sigmoid(x):
    return 1.0 / (1.0 + jnp.exp(-x))


def _ln_in_kernel(x_ref, meta_ref, g_ref, b_ref, o_ref):
    t = pl.program_id(1)

    @pl.when(t == 0)
    def _():
        o_ref[0] = _layer_norm(meta_ref[...], g_ref[...], b_ref[...])

    @pl.when(t > 0)
    def _():
        o_ref[0] = _layer_norm(x_ref[0], g_ref[...], b_ref[...])


def _ln_in(x, meta_pad, g, b, n_blk):
    B = x.shape[0]
    return pl.pallas_call(
        _ln_in_kernel,
        grid=(B, n_blk),
        in_specs=[
            pl.BlockSpec((1, ROW_TILE, D_MODEL), lambda bi, t: (bi, jnp.maximum(t - 1, 0), 0)),
            pl.BlockSpec((ROW_TILE, D_MODEL), lambda bi, t: (0, 0)),
            pl.BlockSpec((1, D_MODEL), lambda bi, t: (0, 0)),
            pl.BlockSpec((1, D_MODEL), lambda bi, t: (0, 0)),
        ],
        out_specs=pl.BlockSpec((1, ROW_TILE, D_MODEL), lambda bi, t: (bi, t, 0)),
        out_shape=jax.ShapeDtypeStruct((B, n_blk * ROW_TILE, D_MODEL), jnp.float32),
        compiler_params=_cparams(("parallel", "parallel")),
        name="ln_in",
    )(x, meta_pad, g, b)


def _proj_kernel(x_ref, w_ref, kvg_ref, lru_ref, g_ref, q_ref, ckv_ref, qidx_ref, kidx_ref, widx_ref,
                 ga_ref, gb_ref):
    xb = x_ref[...].astype(jnp.bfloat16)

    def seg(i, width):
        return jnp.dot(xb, w_ref[:, P_OFFS[i]:P_OFFS[i] + width], preferred_element_type=jnp.float32)

    lru_ref[...] = seg(0, P_LRU)
    g_ref[...] = seg(1, P_G)
    q_ref[...] = seg(2, P_Q).astype(jnp.bfloat16)

    c = seg(3, P_CKV)
    ms = jnp.mean(c * c, axis=-1, keepdims=True)
    ckv_ref[...] = (c * lax.rsqrt(ms + LN_EPS) * kvg_ref[...]).astype(jnp.bfloat16)

    qidx_ref[...] = seg(4, P_QIDX).astype(jnp.bfloat16)

    k = seg(5, P_KIDX)
    lane = lax.broadcasted_iota(jnp.int32, k.shape, 1)
    mu = jnp.sum(k, axis=-1, keepdims=True) * (1.0 / IDX_DIM)
    d = jnp.where(lane < IDX_DIM, k - mu, 0.0)
    var = jnp.sum(d * d, axis=-1, keepdims=True) * (1.0 / IDX_DIM)
    kidx_ref[...] = (d * lax.rsqrt(var + LN_EPS)).astype(jnp.bfloat16)

    widx_ref[...] = seg(6, P_WIDX) * (IDX_HEADS ** -0.5 * IDX_DIM ** -0.5)
    ga_ref[...] = seg(7, P_GA)
    gb_ref[...] = seg(8, P_GB)


def _proj(h2d, w_in_p, kv_g, tm):
    n = h2d.shape[0]
    f32, bf16 = jnp.float32, jnp.bfloat16
    widths = (P_LRU, P_G, P_Q, P_CKV, P_QIDX, P_KIDX, P_WIDX, P_GA, P_GB)
    dtypes = (f32, f32, bf16, bf16, bf16, bf16, f32, f32, f32)
    return pl.pallas_call(
        _proj_kernel,
        grid=(n // tm,),
        in_specs=[
            pl.BlockSpec((tm, D_MODEL), lambda i: (i, 0)),
            _resident((D_MODEL, P_TOTAL)),
            _resident((1, KV_RANK)),
        ],
        out_specs=[pl.BlockSpec((tm, w), lambda i: (i, 0)) for w in widths],
        out_shape=[jax.ShapeDtypeStruct((n, w), dt) for w, dt in zip(widths, dtypes)],
        compiler_params=_cparams(("parallel",)),
        name="proj",
    )(h2d, w_in_p, kv_g)


def _rec_kernel(x_ref, g_ref, cw_ref, cb_ref, wa_ref, ba_ref, wx_ref, bx_ref, lam_ref, o_ref,
                xe_sc, a_sc, u_sc, h_sc):
    tb = pl.program_id(1)

    @pl.when(tb == 0)
    def _():
        xe_sc[0:8, :] = jnp.zeros((8, D_RNN), jnp.float32)
        h_sc[...] = jnp.zeros_like(h_sc)

    row = lax.broadcasted_iota(jnp.int32, (ROW_TILE, 1), 0)
    real = jnp.logical_or(tb > 0, row >= PADF)
    x = jnp.where(real, x_ref[0], 0.0)
    xe_sc[8:8 + ROW_TILE, :] = x
    cw = cw_ref[...]
    xc = (cw[3:4] * x + cw[2:3] * xe_sc[7:7 + ROW_TILE, :] + cw[1:2] * xe_sc[6:6 + ROW_TILE, :]
          + cw[0:1] * xe_sc[5:5 + ROW_TILE, :] + cb_ref[...])
    xe_sc[0:8, :] = xe_sc[ROW_TILE:ROW_TILE + 8, :]

    xcb = xc.astype(jnp.bfloat16)
    lam = lam_ref[...]
    z = -lam
    softplus = jnp.maximum(z, 0.0) + jnp.log1p(jnp.exp(-jnp.abs(z)))
    for n in range(LRU_BLOCKS):
        sl = slice(n * LRU_BLOCK, (n + 1) * LRU_BLOCK)
        xn = xcb[:, sl]
        r = _sigmoid(jnp.dot(xn, wa_ref[n], preferred_element_type=jnp.float32) + ba_ref[:, sl])
        i = _sigmoid(jnp.dot(xn, wx_ref[n], preferred_element_type=jnp.float32) + bx_ref[:, sl])
        log_a = -LRU_C * r * softplus[:, sl]
        a = jnp.exp(log_a)
        a_sc[:, sl] = a
        u = jnp.sqrt(-jnp.tanh(log_a) * (a * a + 1.0)) * (i * xc[:, sl])
        u_sc[:, sl] = jnp.where(real, u, 0.0)

    def step(t, h):
        h = a_sc[pl.ds(t, 1), :] * h + u_sc[pl.ds(t, 1), :]
        u_sc[pl.ds(t, 1), :] = h
        return h

    h_sc[...] = lax.fori_loop(0, ROW_TILE, step, h_sc[...], unroll=8)

    g = g_ref[0]
    gelu = 0.5 * g * (1.0 + jnp.tanh(math.sqrt(2.0 / math.pi) * (g + 0.044715 * (g * g * g))))
    o_ref[0] = (u_sc[...] * gelu).astype(jnp.bfloat16)


def _recurrent(x_lru, g_lru, conv_w, conv_b, w_a, b_a, w_x, b_x, lam):
    B, tp, _ = x_lru.shape
    blk = pl.BlockSpec((1, ROW_TILE, D_RNN), lambda bi, t: (bi, t, 0))
    vec = pl.BlockSpec((1, D_RNN), lambda bi, t: (0, 0))
    wblk = pl.BlockSpec((LRU_BLOCKS, LRU_BLOCK, LRU_BLOCK), lambda bi, t: (0, 0, 0))
    return pl.pallas_call(
        _rec_kernel,
        grid=(B, tp // ROW_TILE),
        in_specs=[blk, blk, pl.BlockSpec((CONV_WIDTH, D_RNN), lambda bi, t: (0, 0)), vec,
                  wblk, vec, wblk, vec, vec],
        out_specs=blk,
        out_shape=jax.ShapeDtypeStruct((B, tp, D_RNN), jnp.bfloat16),
        scratch_shapes=[
            pltpu.VMEM((ROW_TILE + 8, D_RNN), jnp.float32),
            pltpu.VMEM((ROW_TILE, D_RNN), jnp.float32),
            pltpu.VMEM((ROW_TILE, D_RNN), jnp.float32),
            pltpu.VMEM((1, D_RNN), jnp.float32),
        ],
        compiler_params=_cparams(("parallel", "arbitrary")),
        name="recurrent",
    )(x_lru, g_lru, conv_w, conv_b, w_a, b_a, w_x, b_x, lam)


N_COLS = N_HEADS * ROW_TILE
COL_TILE = 2 * ROW_TILE
LOWEST = -3.0e38


def _ordered_bits_to_f32(k):
    return pltpu.bitcast(k ^ (lax.shift_right_arithmetic(k, 31) & 0x7FFFFFFF), jnp.float32)


def _attn_kernel(k_top, qi_ref, wi_ref, q_ref, kidx_ref, ckv_ref, ckvt_ref, wuk_ref, wuv_ref, o_ref,
                 sc_sc, qit_sc, qabst_sc, acct_sc, m_sc, l_sc):
    qb = pl.program_id(1)
    nch = qb // (KEY_CHUNK // ROW_TILE) + 1
    q_pos = qb * ROW_TILE + lax.broadcasted_iota(jnp.int32, (1, ROW_TILE), 1)
    k_iota = lax.broadcasted_iota(jnp.int32, (KEY_CHUNK, 1), 0)

    qit = qi_ref[0].astype(jnp.float32).T
    for h in range(IDX_HEADS):
        qit_sc[:, h * ROW_TILE:(h + 1) * ROW_TILE] = qit[h * IDX_DIM:(h + 1) * IDX_DIM, :].astype(jnp.bfloat16)
    wit = wi_ref[0].T
    wi_row = jnp.concatenate([wit[h:h + 1, :] for h in range(IDX_HEADS)], axis=1)
    qt = q_ref[0].astype(jnp.float32).T
    scale = HEAD_DIM ** -0.5
    for h in range(N_HEADS):
        qa = jnp.dot(wuk_ref[h], qt[h * HEAD_DIM:(h + 1) * HEAD_DIM, :].astype(jnp.bfloat16),
                     preferred_element_type=jnp.float32)
        qabst_sc[:, h * ROW_TILE:(h + 1) * ROW_TILE] = (qa * scale).astype(jnp.bfloat16)

    def score_chunk(c, carry):
        ks = kidx_ref[0, c][:, 0:IDX_DIM]
        sc = None
        for j in range(N_COLS // COL_TILE):
            cols = slice(j * COL_TILE, (j + 1) * COL_TILE)
            lg = jnp.dot(ks, qit_sc[:, cols], preferred_element_type=jnp.float32)
            w = jnp.maximum(lg, 0.0) * wi_row[:, cols]
            part = w[:, :ROW_TILE] + w[:, ROW_TILE:]
            sc = part if sc is None else sc + part
        k_pos = c * KEY_CHUNK + k_iota
        visible = jnp.logical_and(k_pos <= q_pos, k_pos >= PADF)
        sc_sc[c] = jnp.where(visible, sc, -jnp.inf)
        return carry

    lax.fori_loop(0, nch, score_chunk, 0)

    def count(pred):
        def body(c, acc):
            hit = jnp.where(pred(sc_sc[c], c), 1, 0)
            return acc + jnp.sum(hit.reshape(KEY_CHUNK // 8, 8, ROW_TILE), axis=0)
        acc = lax.fori_loop(0, nch, body, jnp.zeros((8, ROW_TILE), jnp.int32))
        return jnp.sum(acc, axis=0, keepdims=True)

    def bit_pass(i, base):
        cand = base + lax.shift_left(jnp.int32(1), 31 - i)
        cand_f = _ordered_bits_to_f32(cand)
        cnt = count(lambda s, c: s >= cand_f)
        return jnp.where(cnt >= k_top, cand, base)

    base = lax.fori_loop(0, 32, bit_pass, jnp.full((1, ROW_TILE), INT_MIN, jnp.int32))
    few = base == INT_MIN
    thr = jnp.where(few, LOWEST, _ordered_bits_to_f32(base))

    n_ge = count(lambda s, c: s >= thr)
    excess = jnp.logical_and(n_ge > k_top, jnp.logical_not(few))

    @pl.when(jnp.max(jnp.where(excess, 1, 0)) > 0)
    def _():
        need = k_top - count(lambda s, c: s > thr)

        def idx_pass(i, p):
            cand = p + lax.shift_left(jnp.int32(1), 13 - i)
            cnt = count(lambda s, c: jnp.logical_and(s == thr, c * KEY_CHUNK + k_iota < cand))
            return jnp.where(cnt < need, cand, p)

        last = lax.fori_loop(0, 14, idx_pass, jnp.zeros((1, ROW_TILE), jnp.int32))

        def demote(c, carry):
            s = sc_sc[c]
            drop = jnp.logical_and(jnp.logical_and(excess, s == thr), c * KEY_CHUNK + k_iota > last)
            sc_sc[c] = jnp.where(drop, -jnp.inf, s)
            return carry

        lax.fori_loop(0, nch, demote, 0)

    m_sc[...] = jnp.full_like(m_sc, NEG_INF)
    l_sc[...] = jnp.zeros_like(l_sc)
    acct_sc[...] = jnp.zeros_like(acct_sc)

    def attn_chunk(c, carry):
        kc = ckv_ref[0, c]
        kct = ckvt_ref[0, c]
        sel = sc_sc[c] >= thr
        for j in range(N_COLS // COL_TILE):
            cols = slice(j * COL_TILE, (j + 1) * COL_TILE)
            s = jnp.dot(kc, qabst_sc[:, cols], preferred_element_type=jnp.float32)
            s = jnp.concatenate([jnp.where(sel, s[:, :ROW_TILE], NEG_INF),
                                 jnp.where(sel, s[:, ROW_TILE:], NEG_INF)], axis=1)
            m_prev = m_sc[:, cols]
            m_new = jnp.maximum(m_prev, jnp.max(s, axis=0, keepdims=True))
            p = jnp.exp(s - m_new)
            alpha = jnp.exp(m_prev - m_new)
            l_sc[:, cols] = alpha * l_sc[:, cols] + jnp.sum(p, axis=0, keepdims=True)
            acct_sc[:, cols] = alpha * acct_sc[:, cols] + jnp.dot(kct, p.astype(jnp.bfloat16),
                                                                  preferred_element_type=jnp.float32)
            m_sc[:, cols] = m_new
        return carry

    lax.fori_loop(0, nch, attn_chunk, 0)

    o_lat_t = acct_sc[...] / l_sc[...]
    for h in range(N_HEADS):
        o_lat = o_lat_t[:, h * ROW_TILE:(h + 1) * ROW_TILE].T.astype(jnp.bfloat16)
        oh = jnp.dot(o_lat, wuv_ref[h], preferred_element_type=jnp.float32)
        o_ref[0, :, h * HEAD_DIM:(h + 1) * HEAD_DIM] = oh.astype(jnp.bfloat16)


def _sparse_attention(qidx, widx, q, kidx_c, ckv_c, ckvt_c, wuk, wuv, k_top):
    B, tp, _ = q.shape
    nch = kidx_c.shape[1]
    return pl.pallas_call(
        functools.partial(_attn_kernel, k_top),
        grid=(B, tp // ROW_TILE),
        in_specs=[
            pl.BlockSpec((1, ROW_TILE, P_QIDX), lambda bi, t: (bi, t, 0)),
            pl.BlockSpec((1, ROW_TILE, P_WIDX), lambda bi, t: (bi, t, 0)),
            pl.BlockSpec((1, ROW_TILE, P_Q), lambda bi, t: (bi, t, 0)),
            pl.BlockSpec((1, nch, KEY_CHUNK, P_KIDX), lambda bi, t: (bi, 0, 0, 0)),
            pl.BlockSpec((1, nch, KEY_CHUNK, P_CKV), lambda bi, t: (bi, 0, 0, 0)),
            pl.BlockSpec((1, nch, P_CKV, KEY_CHUNK), lambda bi, t: (bi, 0, 0, 0)),
            _resident((N_HEADS, KV_RANK, HEAD_DIM)),
            _resident((N_HEADS, KV_RANK, HEAD_DIM)),
        ],
        out_specs=pl.BlockSpec((1, ROW_TILE, P_Q), lambda bi, t: (bi, t, 0)),
        out_shape=jax.ShapeDtypeStruct((B, tp, P_Q), jnp.bfloat16),
        scratch_shapes=[
            pltpu.VMEM((nch, KEY_CHUNK, ROW_TILE), jnp.float32),
            pltpu.VMEM((IDX_DIM, N_COLS), jnp.bfloat16),
            pltpu.VMEM((KV_RANK, N_COLS), jnp.bfloat16),
            pltpu.VMEM((KV_RANK, N_COLS), jnp.float32),
            pltpu.VMEM((1, N_COLS), jnp.float32),
            pltpu.VMEM((1, N_COLS), jnp.float32),
        ],
        compiler_params=_cparams(("parallel", "arbitrary")),
        name="sparse_attn",
    )(qidx, widx, q, kidx_c, ckv_c, ckvt_c, wuk, wuv)


def _merge_kernel(alpha, x_ref, hg_ref, o_ref, ga_ref, gb_ref, wa_ref, wb_ref, wo_ref, g_ref, b_ref, y_ref):
    ya = jnp.dot(hg_ref[...], wa_ref[...], preferred_element_type=jnp.float32)
    yb = jnp.dot(o_ref[...], wb_ref[...], preferred_element_type=jnp.float32)
    mixed = _sigmoid(ga_ref[...]) * ya + _sigmoid(gb_ref[...]) * yb
    z = jnp.dot(mixed.astype(jnp.bfloat16), wo_ref[...], preferred_element_type=jnp.float32)
    y_ref[...] = _layer_norm(alpha * x_ref[...] + z, g_ref[...], b_ref[...])


def _merge(alpha, x2d, hg, o, ga, gb, w_a, w_b, w_o, g, b, tm):
    n = x2d.shape[0]

    def rows(w):
        return pl.BlockSpec((tm, w), lambda i: (i, 0))

    return pl.pallas_call(
        functools.partial(_merge_kernel, alpha),
        grid=(n // tm,),
        in_specs=[rows(D_MODEL), rows(D_RNN), rows(P_Q), rows(D_MODEL), rows(D_MODEL),
                  _resident((D_RNN, D_MODEL)), _resident((P_Q, D_MODEL)), _resident((D_MODEL, D_MODEL)),
                  _resident((1, D_MODEL)), _resident((1, D_MODEL))],
        out_specs=rows(D_MODEL),
        out_shape=jax.ShapeDtypeStruct((n, D_MODEL), jnp.float32),
        compiler_params=_cparams(("parallel",)),
        name="merge",
    )(x2d, hg, o, ga, gb, w_a, w_b, w_o, g, b)


def _mlp_kernel(alpha, x_ref, wu_ref, bu_ref, wd_ref, bd_ref, g_ref, b_ref, y_ref):
    x = x_ref[...]
    h = jnp.dot(x.astype(jnp.bfloat16), wu_ref[...], preferred_element_type=jnp.float32) + bu_ref[...]
    h = jnp.maximum(h, 0.0)
    h = (h * h).astype(jnp.bfloat16)
    z = jnp.dot(h, wd_ref[...], preferred_element_type=jnp.float32) + bd_ref[...]
    y_ref[...] = _layer_norm(alpha * x + z, g_ref[...], b_ref[...])


def _mlp(alpha, x2d, w_up, b_up, w_down, b_down, g, b, tm):
    n = x2d.shape[0]
    return pl.pallas_call(
        functools.partial(_mlp_kernel, alpha),
        grid=(n // tm,),
        in_specs=[pl.BlockSpec((tm, D_MODEL), lambda i: (i, 0)),
                  _resident((D_MODEL, D_FF)), _resident((1, D_FF)),
                  _resident((D_FF, D_MODEL)), _resident((1, D_MODEL)),
                  _resident((1, D_MODEL)), _resident((1, D_MODEL))],
        out_specs=pl.BlockSpec((tm, D_MODEL), lambda i: (i, 0)),
        out_shape=jax.ShapeDtypeStruct((n, D_MODEL), jnp.float32),
        compiler_params=_cparams(("parallel",)),
        name="mlp",
    )(x2d, w_up, b_up, w_down, b_down, g, b)


def _pad_in_proj(w_in):
    splits = (D_RNN, D_RNN, N_HEADS * HEAD_DIM, KV_RANK, IDX_HEADS * IDX_DIM, IDX_DIM, IDX_HEADS,
              D_MODEL, D_MODEL)
    padded = (P_LRU, P_G, P_Q, P_CKV, P_QIDX, P_KIDX, P_WIDX, P_GA, P_GB)
    parts, off = [], 0
    for w, pw in zip(splits, padded):
        part = w_in[:, off:off + w]
        if pw > w:
            part = jnp.pad(part, ((0, 0), (0, pw - w)))
        parts.append(part)
        off += w
    return jnp.concatenate(parts, axis=1).astype(jnp.bfloat16)


def kernel(x, meta_tokens, ln_in_g, ln_in_b, w_in, conv_w, conv_b, w_rg_a, b_rg_a, w_rg_x, b_rg_x, lru_lambda,
           kv_norm_g, w_uk, w_uv, w_branch_a, w_branch_b, w_out, ln1_g, ln1_b, w_up, b_up, w_down, b_down,
           ln2_g, ln2_b):
    B, S, _ = x.shape
    depth = w_in.shape[0]
    assert S % ROW_TILE == 0
    T = S + N_META
    k_top = min(TOPK_MAX, T // 4)
    alpha = (2.0 * depth) ** 0.25
    n_blk = S // ROW_TILE + 1
    tp = n_blk * ROW_TILE
    tk = -(-tp // KEY_CHUNK) * KEY_CHUNK
    n = B * tp
    tm_proj, tm = 320, 640
    assert n % tm_proj == 0 and n % tm == 0
    bf16 = jnp.bfloat16

    def vec(a):
        return a.reshape(1, -1)

    meta_pad = jnp.pad(meta_tokens.astype(x.dtype), ((PADF, 0), (0, 0)))
    h = _ln_in(x, meta_pad, vec(ln_in_g), vec(ln_in_b), n_blk).reshape(n, D_MODEL)

    for l in range(depth):
        lru, g_lru, q, ckv, qidx, kidx, widx, ga, gb = _proj(h, _pad_in_proj(w_in[l]), vec(kv_norm_g[l]), tm_proj)

        hg = _recurrent(lru.reshape(B, tp, D_RNN), g_lru.reshape(B, tp, D_RNN), conv_w[l], vec(conv_b[l]),
                        w_rg_a[l].astype(bf16), vec(b_rg_a[l]), w_rg_x[l].astype(bf16), vec(b_rg_x[l]),
                        vec(lru_lambda[l]))

        key_pad = ((0, 0), (0, tk - tp), (0, 0))
        kidx_c = jnp.pad(kidx.reshape(B, tp, P_KIDX), key_pad).reshape(B, tk // KEY_CHUNK, KEY_CHUNK, P_KIDX)
        ckv_c = jnp.pad(ckv.reshape(B, tp, P_CKV), key_pad).reshape(B, tk // KEY_CHUNK, KEY_CHUNK, P_CKV)
        o = _sparse_attention(qidx.reshape(B, tp, P_QIDX), widx.reshape(B, tp, P_WIDX), q.reshape(B, tp, P_Q),
                              kidx_c, ckv_c, jnp.swapaxes(ckv_c, 2, 3),
                              w_uk[l].astype(bf16), w_uv[l].astype(bf16), k_top)

        h = _merge(alpha, h, hg.reshape(n, D_RNN), o.reshape(n, P_Q), ga, gb, w_branch_a[l].astype(bf16),
                   w_branch_b[l].astype(bf16), w_out[l].astype(bf16), vec(ln1_g[l]), vec(ln1_b[l]), tm)
        h = _mlp(alpha, h, w_up[l].astype(bf16), vec(b_up[l]), w_down[l].astype(bf16), vec(b_down[l]),
                 vec(ln2_g[l]), vec(ln2_b[l]), tm)

    return h.reshape(B, tp, D_MODEL)[:, ROW_TILE:, :]
```

```python
import functools
import math

import jax
import jax.numpy as jnp
from jax import lax
from jax.experimental import pallas as pl
from jax.experimental.pallas import tpu as pltpu

D_MODEL = 1024
N_META = 16
D_RNN = 1280
LRU_BLOCKS = 10
LRU_BLOCK = D_RNN // LRU_BLOCKS
CONV_WIDTH = 4
LRU_C = 8.0
N_HEADS = 8
HEAD_DIM = 128
KV_RANK = 256
IDX_HEADS = 8
IDX_DIM = 64
TOPK_MAX = 256
D_FF = 4 * D_MODEL
LN_EPS = 1e-5
NEG_INF = -1e30

LANES = 128
ROW_TILE = 128
PADF = ROW_TILE - N_META
KEY_CHUNK = 512
VMEM_LIMIT = 56 * 1024 * 1024

P_LRU, P_G, P_Q, P_CKV, P_QIDX, P_KIDX, P_WIDX, P_GA, P_GB = (
    D_RNN, D_RNN, N_HEADS * HEAD_DIM, KV_RANK, IDX_HEADS * IDX_DIM, LANES, LANES, D_MODEL, D_MODEL)
P_OFFS = []
_acc = 0
for _w in (P_LRU, P_G, P_Q, P_CKV, P_QIDX, P_KIDX, P_WIDX, P_GA, P_GB):
    P_OFFS.append(_acc)
    _acc += _w
P_TOTAL = _acc


def _cparams(sem):
    return pltpu.CompilerParams(dimension_semantics=sem, vmem_limit_bytes=VMEM_LIMIT)


def _resident(shape):
    nd = len(shape)
    return pl.BlockSpec(shape, lambda *_: (0,) * nd, pipeline_mode=pl.Buffered(1))


def _layer_norm(x, g, b):
    mu = jnp.mean(x, axis=-1, keepdims=True)
    d = x - mu
    var = jnp.mean(d * d, axis=-1, keepdims=True)
    return d * lax.rsqrt(var + LN_EPS) * g + b


def _sigmoid(x):
    return 1.0 / (1.0 + jnp.exp(-x))


def _ln_in_kernel(x_ref, meta_ref, g_ref, b_ref, o_ref):
    t = pl.program_id(1)

    @pl.when(t == 0)
    def _():
        o_ref[0] = _layer_norm(meta_ref[...], g_ref[...], b_ref[...])

    @pl.when(t > 0)
    def _():
        o_ref[0] = _layer_norm(x_ref[0], g_ref[...], b_ref[...])


def _ln_in(x, meta_pad, g, b, n_blk):
    B = x.shape[0]
    return pl.pallas_call(
        _ln_in_kernel,
        grid=(B, n_blk),
        in_specs=[
            pl.BlockSpec((1, ROW_TILE, D_MODEL), lambda bi, t: (bi, jnp.maximum(t - 1, 0), 0)),
            pl.BlockSpec((ROW_TILE, D_MODEL), lambda bi, t: (0, 0)),
            pl.BlockSpec((1, D_MODEL), lambda bi, t: (0, 0)),
            pl.BlockSpec((1, D_MODEL), lambda bi, t: (0, 0)),
        ],
        out_specs=pl.BlockSpec((1, ROW_TILE, D_MODEL), lambda bi, t: (bi, t, 0)),
        out_shape=jax.ShapeDtypeStruct((B, n_blk * ROW_TILE, D_MODEL), jnp.float32),
        compiler_params=_cparams(("parallel", "parallel")),
        name="ln_in",
    )(x, meta_pad, g, b)


def _proj_kernel(x_ref, w_ref, kvg_ref, lru_ref, g_ref, q_ref, ckv_ref, qidx_ref, kidx_ref, widx_ref,
                 ga_ref, gb_ref):
    xb = x_ref[...].astype(jnp.bfloat16)

    def seg(i, width):
        return jnp.dot(xb, w_ref[:, P_OFFS[i]:P_OFFS[i] + width], preferred_element_type=jnp.float32)

    lru_ref[...] = seg(0, P_LRU)
    g_ref[...] = seg(1, P_G)
    q_ref[...] = seg(2, P_Q).astype(jnp.bfloat16)

    c = seg(3, P_CKV)
    ms = jnp.mean(c * c, axis=-1, keepdims=True)
    ckv_ref[...] = (c * lax.rsqrt(ms + LN_EPS) * kvg_ref[...]).astype(jnp.bfloat16)

    qidx_ref[...] = seg(4, P_QIDX).astype(jnp.bfloat16)

    k = seg(5, P_KIDX)
    lane = lax.broadcasted_iota(jnp.int32, k.shape, 1)
    mu = jnp.sum(k, axis=-1, keepdims=True) * (1.0 / IDX_DIM)
    d = jnp.where(lane < IDX_DIM, k - mu, 0.0)
    var = jnp.sum(d * d, axis=-1, keepdims=True) * (1.0 / IDX_DIM)
    kidx_ref[...] = (d * lax.rsqrt(var + LN_EPS)).astype(jnp.bfloat16)

    widx_ref[...] = seg(6, P_WIDX) * (IDX_HEADS ** -0.5 * IDX_DIM ** -0.5)
    ga_ref[...] = seg(7, P_GA)
    gb_ref[...] = seg(8, P_GB)


def _proj(h2d, w_in_p, kv_g, tm):
    n = h2d.shape[0]
    f32, bf16 = jnp.float32, jnp.bfloat16
    widths = (P_LRU, P_G, P_Q, P_CKV, P_QIDX, P_KIDX, P_WIDX, P_GA, P_GB)
    dtypes = (f32, f32, bf16, bf16, bf16, bf16, f32, f32, f32)
    return pl.pallas_call(
        _proj_kernel,
        grid=(n // tm,),
        in_specs=[
            pl.BlockSpec((tm, D_MODEL), lambda i: (i, 0)),
            _resident((D_MODEL, P_TOTAL)),
            _resident((1, KV_RANK)),
        ],
        out_specs=[pl.BlockSpec((tm, w), lambda i: (i, 0)) for w in widths],
        out_shape=[jax.ShapeDtypeStruct((n, w), dt) for w, dt in zip(widths, dtypes)],
        compiler_params=_cparams(("parallel",)),
        name="proj",
    )(h2d, w_in_p, kv_g)


def _rec_kernel(x_ref, g_ref, cw_ref, cb_ref, wa_ref, ba_ref, wx_ref, bx_ref, lam_ref, o_ref,
                xe_sc, a_sc, u_sc, h_sc):
    tb = pl.program_id(1)

    @pl.when(tb == 0)
    def _():
        xe_sc[0:8, :] = jnp.zeros((8, D_RNN), jnp.float32)
        h_sc[...] = jnp.zeros_like(h_sc)

    row = lax.broadcasted_iota(jnp.int32, (ROW_TILE, 1), 0)
    real = jnp.logical_or(tb > 0, row >= PADF)
    x = jnp.where(real, x_ref[0], 0.0)
    xe_sc[8:8 + ROW_TILE, :] = x
    cw = cw_ref[...]
    xc = (cw[3:4] * x + cw[2:3] * xe_sc[7:7 + ROW_TILE, :] + cw[1:2] * xe_sc[6:6 + ROW_TILE, :]
          + cw[0:1] * xe_sc[5:5 + ROW_TILE, :] + cb_ref[...])
    xe_sc[0:8, :] = xe_sc[ROW_TILE:ROW_TILE + 8, :]

    xcb = xc.astype(jnp.bfloat16)
    lam = lam_ref[...]
    z = -lam
    softplus = jnp.maximum(z, 0.0) + jnp.log1p(jnp.exp(-jnp.abs(z)))
    for n in range(LRU_BLOCKS):
        sl = slice(n * LRU_BLOCK, (n + 1) * LRU_BLOCK)
        xn = xcb[:, sl]
        r = _sigmoid(jnp.dot(xn, wa_ref[n], preferred_element_type=jnp.float32) + ba_ref[:, sl])
        i = _sigmoid(jnp.dot(xn, wx_ref[n], preferred_element_type=jnp.float32) + bx_ref[:, sl])
        log_a = -LRU_C * r * softplus[:, sl]
        a = jnp.exp(log_a)
        a_sc[:, sl] = a
        u = jnp.sqrt(-jnp.tanh(log_a) * (a * a + 1.0)) * (i * xc[:, sl])
        u_sc[:, sl] = jnp.where(real, u, 0.0)

    def step(t, h):
        h = a_sc[pl.ds(t, 1), :] * h + u_sc[pl.ds(t, 1), :]
        u_sc[pl.ds(t, 1), :] = h
        return h

    h_sc[...] = lax.fori_loop(0, ROW_TILE, step, h_sc[...], unroll=8)

    g = g_ref[0]
    gelu = 0.5 * g * (1.0 + jnp.tanh(math.sqrt(2.0 / math.pi) * (g + 0.044715 * (g * g * g))))
    o_ref[0] = (u_sc[...] * gelu).astype(jnp.bfloat16)


def _recurrent(x_lru, g_lru, conv_w, conv_b, w_a, b_a, w_x, b_x, lam):
    B, tp, _ = x_lru.shape
    blk = pl.BlockSpec((1, ROW_TILE, D_RNN), lambda bi, t: (bi, t, 0))
    vec = pl.BlockSpec((1, D_RNN), lambda bi, t: (0, 0))
    wblk = pl.BlockSpec((LRU_BLOCKS, LRU_BLOCK, LRU_BLOCK), lambda bi, t: (0, 0, 0))
    return pl.pallas_call(
        _rec_kernel,
        grid=(B, tp // ROW_TILE),
        in_specs=[blk, blk, pl.BlockSpec((CONV_WIDTH, D_RNN), lambda bi, t: (0, 0)), vec,
                  wblk, vec, wblk, vec, vec],
        out_specs=blk,
        out_shape=jax.ShapeDtypeStruct((B, tp, D_RNN), jnp.bfloat16),
        scratch_shapes=[
            pltpu.VMEM((ROW_TILE + 8, D_RNN), jnp.float32),
            pltpu.VMEM((ROW_TILE, D_RNN), jnp.float32),
            pltpu.VMEM((ROW_TILE, D_RNN), jnp.float32),
            pltpu.VMEM((1, D_RNN), jnp.float32),
        ],
        compiler_params=_cparams(("parallel", "arbitrary")),
        name="recurrent",
    )(x_lru, g_lru, conv_w, conv_b, w_a, b_a, w_x, b_x, lam)


N_COLS = N_HEADS * ROW_TILE
COL_TILE = 2 * ROW_TILE
LOWEST = -3.0e38


def _ordered_bits_to_f32(k):
    return pltpu.bitcast(k ^ (lax.shift_right_arithmetic(k, 31) & 0x7FFFFFFF), jnp.float32)


def _attn_kernel(k_top, qi_ref, wi_ref, q_ref, kidx_ref, ckv_ref, ckvt_ref, wuk_ref, wuv_ref, o_ref,
                 sc_sc, sc16_sc, qit_sc, qabst_sc, acct_sc, m_sc, l_sc, alpha_sc, p_sc, s0_sc):
    qb = pl.program_id(1)
    n_chunks = sc_sc.shape[0]
    nch = qb // (KEY_CHUNK // ROW_TILE) + 1
    q_pos = qb * ROW_TILE + lax.broadcasted_iota(jnp.int32, (1, ROW_TILE), 1)
    k_iota = lax.broadcasted_iota(jnp.int32, (KEY_CHUNK, 1), 0)

    qit = qi_ref[0].astype(jnp.float32).T
    for h in range(IDX_HEADS):
        qit_sc[:, h * ROW_TILE:(h + 1) * ROW_TILE] = qit[h * IDX_DIM:(h + 1) * IDX_DIM, :].astype(jnp.bfloat16)
    wit = wi_ref[0].T
    wi_row = jnp.concatenate([wit[h:h + 1, :] for h in range(IDX_HEADS)], axis=1)
    qt = q_ref[0].astype(jnp.float32).T
    scale = HEAD_DIM ** -0.5 * math.log2(math.e)
    for h in range(N_HEADS):
        qa = jnp.dot(wuk_ref[h], qt[h * HEAD_DIM:(h + 1) * HEAD_DIM, :].astype(jnp.bfloat16),
                     preferred_element_type=jnp.float32)
        qabst_sc[:, h * ROW_TILE:(h + 1) * ROW_TILE] = (qa * scale).astype(jnp.bfloat16)

    def score_chunk(c, carry):
        ks = kidx_ref[0, c][:, 0:IDX_DIM]
        sc = None
        for j in range(N_COLS // COL_TILE):
            cols = slice(j * COL_TILE, (j + 1) * COL_TILE)
            lg = jnp.dot(ks, qit_sc[:, cols], preferred_element_type=jnp.float32)
            w = jnp.maximum(lg, 0.0) * wi_row[:, cols]
            part = w[:, :ROW_TILE] + w[:, ROW_TILE:]
            sc = part if sc is None else sc + part
        k_pos = c * KEY_CHUNK + k_iota
        visible = jnp.logical_and(k_pos <= q_pos, k_pos >= PADF)
        sc = jnp.where(visible, sc, -jnp.inf)
        sc_sc[c] = sc
        sc16_sc[c] = sc.astype(jnp.bfloat16)
        return carry

    lax.fori_loop(0, nch, score_chunk, 0)

    def count(pred):
        def body(c, acc):
            hit = jnp.where(pred(sc_sc[c], c), 1, 0)
            return acc + jnp.sum(hit.reshape(KEY_CHUNK // 8, 8, ROW_TILE), axis=0)
        acc = lax.fori_loop(0, nch, body, jnp.zeros((8, ROW_TILE), jnp.int32))
        return jnp.sum(acc, axis=0, keepdims=True)

    def count16(cand):
        one = jnp.ones((), jnp.bfloat16)
        zero = jnp.zeros((), jnp.bfloat16)

        def body(c, acc):
            hit = jnp.where(sc16_sc[c] >= cand, one, zero).reshape(KEY_CHUNK // 16, 16, ROW_TILE)
            parts = [hit[i] for i in range(KEY_CHUNK // 16)]
            while len(parts) > 1:
                parts = [parts[i] + parts[i + 1] for i in range(0, len(parts), 2)]
            return acc + parts[0].astype(jnp.float32)
        acc = lax.fori_loop(0, nch, body, jnp.zeros((16, ROW_TILE), jnp.float32))
        return jnp.sum(acc, axis=0, keepdims=True)

    def coarse_pass(i, base):
        cand = base + lax.shift_left(jnp.int32(1), 15 - i)
        raw = cand ^ (lax.shift_right_arithmetic(cand, 15) & 0x7FFF)
        cand_f = pltpu.bitcast(lax.shift_left(raw, 16), jnp.float32)
        cnt = count16(cand_f.astype(jnp.bfloat16))
        return jnp.where(cnt >= k_top, cand, base)

    v16 = lax.fori_loop(0, 16, coarse_pass, jnp.full((1, ROW_TILE), -(2 ** 15), jnp.int32))
    few = v16 == -(2 ** 15)

    lo_k = lax.shift_left(v16 - 1, 16)
    span = 3 * 2 ** 16

    def fine_pass(i, state):
        off, n_ge = state
        cand_off = off + lax.shift_left(jnp.int32(1), 17 - i)
        cand_f = _ordered_bits_to_f32(lo_k + cand_off)
        cnt = count(lambda s, c: s >= cand_f)
        keep = jnp.logical_and(cnt >= k_top, cand_off < span)
        return jnp.where(keep, cand_off, off), jnp.where(keep, cnt, n_ge)

    off, n_ge = lax.fori_loop(0, 18, fine_pass, (jnp.zeros((1, ROW_TILE), jnp.int32),
                                                 jnp.zeros((1, ROW_TILE), jnp.int32)))
    thr = jnp.where(few, LOWEST, _ordered_bits_to_f32(lo_k + off))

    excess = jnp.logical_and(n_ge > k_top, jnp.logical_not(few))

    @pl.when(jnp.max(jnp.where(excess, 1, 0)) > 0)
    def _():
        need = k_top - count(lambda s, c: s > thr)

        def idx_pass(i, p):
            cand = p + lax.shift_left(jnp.int32(1), 13 - i)
            cnt = count(lambda s, c: jnp.logical_and(s == thr, c * KEY_CHUNK + k_iota < cand))
            return jnp.where(cnt < need, cand, p)

        last = lax.fori_loop(0, 14, idx_pass, jnp.zeros((1, ROW_TILE), jnp.int32))

        def demote(c, carry):
            s = sc_sc[c]
            drop = jnp.logical_and(jnp.logical_and(excess, s == thr), c * KEY_CHUNK + k_iota > last)
            sc_sc[c] = jnp.where(drop, -jnp.inf, s)
            return carry

        lax.fori_loop(0, nch, demote, 0)

    n_tiles = N_COLS // COL_TILE
    last = slice((n_tiles - 1) * COL_TILE, n_tiles * COL_TILE)
    m_sc[...] = jnp.full_like(m_sc, NEG_INF)
    l_sc[...] = jnp.zeros_like(l_sc)
    acct_sc[...] = jnp.zeros_like(acct_sc)
    alpha_sc[:, last] = jnp.ones((1, COL_TILE), jnp.float32)
    p_sc[:, last] = jnp.zeros((KEY_CHUNK, COL_TILE), jnp.bfloat16)

    def logits_tile(kc, j):
        cols = slice(j * COL_TILE, (j + 1) * COL_TILE)
        return jnp.dot(kc, qabst_sc[:, cols], preferred_element_type=jnp.float32)

    def softmax_tile(s, sel, j):
        cols = slice(j * COL_TILE, (j + 1) * COL_TILE)
        s = jnp.concatenate([jnp.where(sel, s[:, :ROW_TILE], NEG_INF),
                             jnp.where(sel, s[:, ROW_TILE:], NEG_INF)], axis=1)
        m_prev = m_sc[:, cols]
        m_new = jnp.maximum(m_prev, jnp.max(s, axis=0, keepdims=True))
        p = jnp.exp2(s - m_new)
        alpha = jnp.exp2(m_prev - m_new)
        l_sc[:, cols] = alpha * l_sc[:, cols] + jnp.sum(p, axis=0, keepdims=True)
        m_sc[:, cols] = m_new
        alpha_sc[:, cols] = alpha
        p_sc[:, cols] = p.astype(jnp.bfloat16)

    def value_tile(kct, j):
        cols = slice(j * COL_TILE, (j + 1) * COL_TILE)
        acct_sc[:, cols] = alpha_sc[:, cols] * acct_sc[:, cols] + jnp.dot(
            kct, p_sc[:, cols], preferred_element_type=jnp.float32)

    def attn_chunk(c, carry):
        kc = ckv_ref[0, c]
        kct = ckvt_ref[0, c]
        sel = sc_sc[c] >= thr
        softmax_tile(s0_sc[...], sel, 0)
        value_tile(ckvt_ref[0, jnp.maximum(c - 1, 0)], n_tiles - 1)
        for j in range(1, n_tiles):
            softmax_tile(logits_tile(kc, j), sel, j)
            value_tile(kct, j - 1)
        s0_sc[...] = logits_tile(ckv_ref[0, jnp.minimum(c + 1, n_chunks - 1)], 0)
        return carry

    s0_sc[...] = logits_tile(ckv_ref[0, 0], 0)
    lax.fori_loop(0, nch, attn_chunk, 0)
    value_tile(ckvt_ref[0, nch - 1], n_tiles - 1)

    o_lat_t = acct_sc[...] / l_sc[...]
    for h in range(N_HEADS):
        o_lat = o_lat_t[:, h * ROW_TILE:(h + 1) * ROW_TILE].T.astype(jnp.bfloat16)
        oh = jnp.dot(o_lat, wuv_ref[h], preferred_element_type=jnp.float32)
        o_ref[0, :, h * HEAD_DIM:(h + 1) * HEAD_DIM] = oh.astype(jnp.bfloat16)


def _sparse_attention(qidx, widx, q, kidx_c, ckv_c, ckvt_c, wuk, wuv, k_top):
    B, tp, _ = q.shape
    nch = kidx_c.shape[1]
    return pl.pallas_call(
        functools.partial(_attn_kernel, k_top),
        grid=(B, tp // ROW_TILE),
        in_specs=[
            pl.BlockSpec((1, ROW_TILE, P_QIDX), lambda bi, t: (bi, t, 0)),
            pl.BlockSpec((1, ROW_TILE, P_WIDX), lambda bi, t: (bi, t, 0)),
            pl.BlockSpec((1, ROW_TILE, P_Q), lambda bi, t: (bi, t, 0)),
            pl.BlockSpec((1, nch, KEY_CHUNK, P_KIDX), lambda bi, t: (bi, 0, 0, 0)),
            pl.BlockSpec((1, nch, KEY_CHUNK, P_CKV), lambda bi, t: (bi, 0, 0, 0)),
            pl.BlockSpec((1, nch, P_CKV, KEY_CHUNK), lambda bi, t: (bi, 0, 0, 0)),
            _resident((N_HEADS, KV_RANK, HEAD_DIM)),
            _resident((N_HEADS, KV_RANK, HEAD_DIM)),
        ],
        out_specs=pl.BlockSpec((1, ROW_TILE, P_Q), lambda bi, t: (bi, t, 0)),
        out_shape=jax.ShapeDtypeStruct((B, tp, P_Q), jnp.bfloat16),
        scratch_shapes=[
            pltpu.VMEM((nch, KEY_CHUNK, ROW_TILE), jnp.float32),
            pltpu.VMEM((nch, KEY_CHUNK, ROW_TILE), jnp.bfloat16),
            pltpu.VMEM((IDX_DIM, N_COLS), jnp.bfloat16),
            pltpu.VMEM((KV_RANK, N_COLS), jnp.bfloat16),
            pltpu.VMEM((KV_RANK, N_COLS), jnp.float32),
            pltpu.VMEM((1, N_COLS), jnp.float32),
            pltpu.VMEM((1, N_COLS), jnp.float32),
            pltpu.VMEM((1, N_COLS), jnp.float32),
            pltpu.VMEM((KEY_CHUNK, N_COLS), jnp.bfloat16),
            pltpu.VMEM((KEY_CHUNK, COL_TILE), jnp.float32),
        ],
        compiler_params=_cparams(("parallel", "arbitrary")),
        name="sparse_attn",
    )(qidx, widx, q, kidx_c, ckv_c, ckvt_c, wuk, wuv)


def _merge_kernel(alpha, x_ref, hg_ref, o_ref, ga_ref, gb_ref, wa_ref, wb_ref, wo_ref, g_ref, b_ref, y_ref):
    ya = jnp.dot(hg_ref[...], wa_ref[...], preferred_element_type=jnp.float32)
    yb = jnp.dot(o_ref[...], wb_ref[...], preferred_element_type=jnp.float32)
    mixed = _sigmoid(ga_ref[...]) * ya + _sigmoid(gb_ref[...]) * yb
    z = jnp.dot(mixed.astype(jnp.bfloat16), wo_ref[...], preferred_element_type=jnp.float32)
    y_ref[...] = _layer_norm(alpha * x_ref[...] + z, g_ref[...], b_ref[...])


def _merge(alpha, x2d, hg, o, ga, gb, w_a, w_b, w_o, g, b, tm):
    n = x2d.shape[0]

    def rows(w):
        return pl.BlockSpec((tm, w), lambda i: (i, 0))

    return pl.pallas_call(
        functools.partial(_merge_kernel, alpha),
        grid=(n // tm,),
        in_specs=[rows(D_MODEL), rows(D_RNN), rows(P_Q), rows(D_MODEL), rows(D_MODEL),
                  _resident((D_RNN, D_MODEL)), _resident((P_Q, D_MODEL)), _resident((D_MODEL, D_MODEL)),
                  _resident((1, D_MODEL)), _resident((1, D_MODEL))],
        out_specs=rows(D_MODEL),
        out_shape=jax.ShapeDtypeStruct((n, D_MODEL), jnp.float32),
        compiler_params=_cparams(("parallel",)),
        name="merge",
    )(x2d, hg, o, ga, gb, w_a, w_b, w_o, g, b)


def _mlp_kernel(alpha, x_ref, wu_ref, bu_ref, wd_ref, bd_ref, g_ref, b_ref, y_ref):
    x = x_ref[...]
    h = jnp.dot(x.astype(jnp.bfloat16), wu_ref[...], preferred_element_type=jnp.float32) + bu_ref[...]
    h = jnp.maximum(h, 0.0)
    h = (h * h).astype(jnp.bfloat16)
    z = jnp.dot(h, wd_ref[...], preferred_element_type=jnp.float32) + bd_ref[...]
    y_ref[...] = _layer_norm(alpha * x + z, g_ref[...], b_ref[...])


def _mlp(alpha, x2d, w_up, b_up, w_down, b_down, g, b, tm):
    n = x2d.shape[0]
    return pl.pallas_call(
        functools.partial(_mlp_kernel, alpha),
        grid=(n // tm,),
        in_specs=[pl.BlockSpec((tm, D_MODEL), lambda i: (i, 0)),
                  _resident((D_MODEL, D_FF)), _resident((1, D_FF)),
                  _resident((D_FF, D_MODEL)), _resident((1, D_MODEL)),
                  _resident((1, D_MODEL)), _resident((1, D_MODEL))],
        out_specs=pl.BlockSpec((tm, D_MODEL), lambda i: (i, 0)),
        out_shape=jax.ShapeDtypeStruct((n, D_MODEL), jnp.float32),
        compiler_params=_cparams(("parallel",)),
        name="mlp",
    )(x2d, w_up, b_up, w_down, b_down, g, b)


def _pad_in_proj(w_in):
    splits = (D_RNN, D_RNN, N_HEADS * HEAD_DIM, KV_RANK, IDX_HEADS * IDX_DIM, IDX_DIM, IDX_HEADS,
              D_MODEL, D_MODEL)
    padded = (P_LRU, P_G, P_Q, P_CKV, P_QIDX, P_KIDX, P_WIDX, P_GA, P_GB)
    parts, off = [], 0
    for w, pw in zip(splits, padded):
        part = w_in[:, off:off + w]
        if pw > w:
            part = jnp.pad(part, ((0, 0), (0, pw - w)))
        parts.append(part)
        off += w
    return jnp.concatenate(parts, axis=1).astype(jnp.bfloat16)


def kernel(x, meta_tokens, ln_in_g, ln_in_b, w_in, conv_w, conv_b, w_rg_a, b_rg_a, w_rg_x, b_rg_x, lru_lambda,
           kv_norm_g, w_uk, w_uv, w_branch_a, w_branch_b, w_out, ln1_g, ln1_b, w_up, b_up, w_down, b_down,
           ln2_g, ln2_b):
    B, S, _ = x.shape
    depth = w_in.shape[0]
    assert S % ROW_TILE == 0
    T = S + N_META
    k_top = min(TOPK_MAX, T // 4)
    alpha = (2.0 * depth) ** 0.25
    n_blk = S // ROW_TILE + 1
    tp = n_blk * ROW_TILE
    tk = -(-tp // KEY_CHUNK) * KEY_CHUNK
    n = B * tp
    tm_proj, tm = 320, 640
    assert n % tm_proj == 0 and n % tm == 0
    bf16 = jnp.bfloat16

    def vec(a):
        return a.reshape(1, -1)

    meta_pad = jnp.pad(meta_tokens.astype(x.dtype), ((PADF, 0), (0, 0)))
    h = _ln_in(x, meta_pad, vec(ln_in_g), vec(ln_in_b), n_blk).reshape(n, D_MODEL)

    for l in range(depth):
        lru, g_lru, q, ckv, qidx, kidx, widx, ga, gb = _proj(h, _pad_in_proj(w_in[l]), vec(kv_norm_g[l]), tm_proj)

        hg = _recurrent(lru.reshape(B, tp, D_RNN), g_lru.reshape(B, tp, D_RNN), conv_w[l], vec(conv_b[l]),
                        w_rg_a[l].astype(bf16), vec(b_rg_a[l]), w_rg_x[l].astype(bf16), vec(b_rg_x[l]),
                        vec(lru_lambda[l]))

        key_pad = ((0, 0), (0, tk - tp), (0, 0))
        kidx_c = jnp.pad(kidx.reshape(B, tp, P_KIDX), key_pad).reshape(B, tk // KEY_CHUNK, KEY_CHUNK, P_KIDX)
        ckv_c = jnp.pad(ckv.reshape(B, tp, P_CKV), key_pad).reshape(B, tk // KEY_CHUNK, KEY_CHUNK, P_CKV)
        o = _sparse_attention(qidx.reshape(B, tp, P_QIDX), widx.reshape(B, tp, P_WIDX), q.reshape(B, tp, P_Q),
                              kidx_c, ckv_c, jnp.swapaxes(ckv_c, 2, 3),
                              w_uk[l].astype(bf16), w_uv[l].astype(bf16), k_top)

        h = _merge(alpha, h, hg.reshape(n, D_RNN), o.reshape(n, P_Q), ga, gb, w_branch_a[l].astype(bf16),
                   w_branch_b[l].astype(bf16), w_out[l].astype(bf16), vec(ln1_g[l]), vec(ln1_b[l]), tm)
        h = _mlp(alpha, h, w_up[l].astype(bf16), vec(b_up[l]), w_down[l].astype(bf16), vec(b_down[l]),
                 vec(ln2_g[l]), vec(ln2_b[l]), tm)

    return h.reshape(B, tp, D_MODEL)[:, ROW_TILE:, :]
```

```python
import functools
import math

import jax
import jax.numpy as jnp
from jax import lax
from jax.experimental import pallas as pl
from jax.experimental.pallas import tpu as pltpu

D_MODEL = 1024
N_META = 16
D_RNN = 1280
LRU_BLOCKS = 10
LRU_BLOCK = D_RNN // LRU_BLOCKS
CONV_WIDTH = 4
LRU_C = 8.0
N_HEADS = 8
HEAD_DIM = 128
KV_RANK = 256
IDX_HEADS = 8
IDX_DIM = 64
TOPK_MAX = 256
D_FF = 4 * D_MODEL
LN_EPS = 1e-5
NEG_INF = -1e30

LANES = 128
ROW_TILE = 128
PADF = ROW_TILE - N_META
KEY_CHUNK = 512
VMEM_LIMIT = 56 * 1024 * 1024

P_LRU, P_G, P_Q, P_CKV, P_QIDX, P_KIDX, P_WIDX, P_GA, P_GB = (
    D_RNN, D_RNN, N_HEADS * HEAD_DIM, KV_RANK, IDX_HEADS * IDX_DIM, LANES, LANES, D_MODEL, D_MODEL)
P_OFFS = []
_acc = 0
for _w in (P_LRU, P_G, P_Q, P_CKV, P_QIDX, P_KIDX, P_WIDX, P_GA, P_GB):
    P_OFFS.append(_acc)
    _acc += _w
P_TOTAL = _acc


def _cparams(sem):
    return pltpu.CompilerParams(dimension_semantics=sem, vmem_limit_bytes=VMEM_LIMIT)


def _resident(shape):
    nd = len(shape)
    return pl.BlockSpec(shape, lambda *_: (0,) * nd, pipeline_mode=pl.Buffered(1))


def _layer_norm(x, g, b):
    mu = jnp.mean(x, axis=-1, keepdims=True)
    d = x - mu
    var = jnp.mean(d * d, axis=-1, keepdims=True)
    return d * lax.rsqrt(var + LN_EPS) * g + b


def _sigmoid(x):
    return 1.0 / (1.0 + jnp.exp(-x))


def _ln_in_kernel(x_ref, meta_ref, g_ref, b_ref, o_ref):
    t = pl.program_id(1)

    @pl.when(t == 0)
    def _():
        o_ref[0] = _layer_norm(meta_ref[...], g_ref[...], b_ref[...])

    @pl.when(t > 0)
    def _():
        o_ref[0] = _layer_norm(x_ref[0], g_ref[...], b_ref[...])


def _ln_in(x, meta_pad, g, b, n_blk):
    B = x.shape[0]
    return pl.pallas_call(
        _ln_in_kernel,
        grid=(B, n_blk),
        in_specs=[
            pl.BlockSpec((1, ROW_TILE, D_MODEL), lambda bi, t: (bi, jnp.maximum(t - 1, 0), 0)),
            pl.BlockSpec((ROW_TILE, D_MODEL), lambda bi, t: (0, 0)),
            pl.BlockSpec((1, D_MODEL), lambda bi, t: (0, 0)),
            pl.BlockSpec((1, D_MODEL), lambda bi, t: (0, 0)),
        ],
        out_specs=pl.BlockSpec((1, ROW_TILE, D_MODEL), lambda bi, t: (bi, t, 0)),
        out_shape=jax.ShapeDtypeStruct((B, n_blk * ROW_TILE, D_MODEL), jnp.float32),
        compiler_params=_cparams(("parallel", "parallel")),
        name="ln_in",
    )(x, meta_pad, g, b)


def _proj_kernel(x_ref, w_ref, kvg_ref, lru_ref, g_ref, q_ref, ckv_ref, qidx_ref, kidx_ref, widx_ref,
                 ga_ref, gb_ref):
    xb = x_ref[...].astype(jnp.bfloat16)

    def seg(i, width):
        return jnp.dot(xb, w_ref[:, P_OFFS[i]:P_OFFS[i] + width], preferred_element_type=jnp.float32)

    lru_ref[...] = seg(0, P_LRU)
    g_ref[...] = seg(1, P_G)
    q_ref[...] = seg(2, P_Q).astype(jnp.bfloat16)

    c = seg(3, P_CKV)
    ms = jnp.mean(c * c, axis=-1, keepdims=True)
    ckv_ref[...] = (c * lax.rsqrt(ms + LN_EPS) * kvg_ref[...]).astype(jnp.bfloat16)

    qidx_ref[...] = seg(4, P_QIDX).astype(jnp.bfloat16)

    k = seg(5, P_KIDX)
    lane = lax.broadcasted_iota(jnp.int32, k.shape, 1)
    mu = jnp.sum(k, axis=-1, keepdims=True) * (1.0 / IDX_DIM)
    d = jnp.where(lane < IDX_DIM, k - mu, 0.0)
    var = jnp.sum(d * d, axis=-1, keepdims=True) * (1.0 / IDX_DIM)
    kidx_ref[...] = (d * lax.rsqrt(var + LN_EPS)).astype(jnp.bfloat16)

    widx_ref[...] = seg(6, P_WIDX) * (IDX_HEADS ** -0.5 * IDX_DIM ** -0.5)
    ga_ref[...] = seg(7, P_GA)
    gb_ref[...] = seg(8, P_GB)


def _proj(h2d, w_in_p, kv_g, tm):
    n = h2d.shape[0]
    f32, bf16 = jnp.float32, jnp.bfloat16
    widths = (P_LRU, P_G, P_Q, P_CKV, P_QIDX, P_KIDX, P_WIDX, P_GA, P_GB)
    dtypes = (f32, f32, bf16, bf16, bf16, bf16, f32, f32, f32)
    return pl.pallas_call(
        _proj_kernel,
        grid=(n // tm,),
        in_specs=[
            pl.BlockSpec((tm, D_MODEL), lambda i: (i, 0)),
            _resident((D_MODEL, P_TOTAL)),
            _resident((1, KV_RANK)),
        ],
        out_specs=[pl.BlockSpec((tm, w), lambda i: (i, 0)) for w in widths],
        out_shape=[jax.ShapeDtypeStruct((n, w), dt) for w, dt in zip(widths, dtypes)],
        compiler_params=_cparams(("parallel",)),
        name="proj",
    )(h2d, w_in_p, kv_g)


def _rec_kernel(x_ref, g_ref, cw_ref, cb_ref, wa_ref, ba_ref, wx_ref, bx_ref, lam_ref, o_ref,
                xe_sc, a_sc, u_sc, h_sc):
    tb = pl.program_id(1)

    @pl.when(tb == 0)
    def _():
        xe_sc[0:8, :] = jnp.zeros((8, D_RNN), jnp.float32)
        h_sc[...] = jnp.zeros_like(h_sc)

    row = lax.broadcasted_iota(jnp.int32, (ROW_TILE, 1), 0)
    real = jnp.logical_or(tb > 0, row >= PADF)
    x = jnp.where(real, x_ref[0], 0.0)
    xe_sc[8:8 + ROW_TILE, :] = x
    cw = cw_ref[...]
    xc = (cw[3:4] * x + cw[2:3] * xe_sc[7:7 + ROW_TILE, :] + cw[1:2] * xe_sc[6:6 + ROW_TILE, :]
          + cw[0:1] * xe_sc[5:5 + ROW_TILE, :] + cb_ref[...])
    xe_sc[0:8, :] = xe_sc[ROW_TILE:ROW_TILE + 8, :]

    xcb = xc.astype(jnp.bfloat16)
    lam = lam_ref[...]
    z = -lam
    softplus = jnp.maximum(z, 0.0) + jnp.log1p(jnp.exp(-jnp.abs(z)))
    for n in range(LRU_BLOCKS):
        sl = slice(n * LRU_BLOCK, (n + 1) * LRU_BLOCK)
        xn = xcb[:, sl]
        r = _sigmoid(jnp.dot(xn, wa_ref[n], preferred_element_type=jnp.float32) + ba_ref[:, sl])
        i = _sigmoid(jnp.dot(xn, wx_ref[n], preferred_element_type=jnp.float32) + bx_ref[:, sl])
        log_a = -LRU_C * r * softplus[:, sl]
        a = jnp.exp(log_a)
        a_sc[:, sl] = a
        u = jnp.sqrt(-jnp.tanh(log_a) * (a * a + 1.0)) * (i * xc[:, sl])
        u_sc[:, sl] = jnp.where(real, u, 0.0)

    def step(t, h):
        h = a_sc[pl.ds(t, 1), :] * h + u_sc[pl.ds(t, 1), :]
        u_sc[pl.ds(t, 1), :] = h
        return h

    h_sc[...] = lax.fori_loop(0, ROW_TILE, step, h_sc[...], unroll=8)

    g = g_ref[0]
    gelu = 0.5 * g * (1.0 + jnp.tanh(math.sqrt(2.0 / math.pi) * (g + 0.044715 * (g * g * g))))
    o_ref[0] = (u_sc[...] * gelu).astype(jnp.bfloat16)


def _recurrent(x_lru, g_lru, conv_w, conv_b, w_a, b_a, w_x, b_x, lam):
    B, tp, _ = x_lru.shape
    blk = pl.BlockSpec((1, ROW_TILE, D_RNN), lambda bi, t: (bi, t, 0))
    vec = pl.BlockSpec((1, D_RNN), lambda bi, t: (0, 0))
    wblk = pl.BlockSpec((LRU_BLOCKS, LRU_BLOCK, LRU_BLOCK), lambda bi, t: (0, 0, 0))
    return pl.pallas_call(
        _rec_kernel,
        grid=(B, tp // ROW_TILE),
        in_specs=[blk, blk, pl.BlockSpec((CONV_WIDTH, D_RNN), lambda bi, t: (0, 0)), vec,
                  wblk, vec, wblk, vec, vec],
        out_specs=blk,
        out_shape=jax.ShapeDtypeStruct((B, tp, D_RNN), jnp.bfloat16),
        scratch_shapes=[
            pltpu.VMEM((ROW_TILE + 8, D_RNN), jnp.float32),
            pltpu.VMEM((ROW_TILE, D_RNN), jnp.float32),
            pltpu.VMEM((ROW_TILE, D_RNN), jnp.float32),
            pltpu.VMEM((1, D_RNN), jnp.float32),
        ],
        compiler_params=_cparams(("parallel", "arbitrary")),
        name="recurrent",
    )(x_lru, g_lru, conv_w, conv_b, w_a, b_a, w_x, b_x, lam)


N_COLS = N_HEADS * ROW_TILE
COL_TILE = 2 * ROW_TILE
LOWEST = -3.0e38


def _ordered_bits_to_f32(k):
    return pltpu.bitcast(k ^ (lax.shift_right_arithmetic(k, 31) & 0x7FFFFFFF), jnp.float32)


def _attn_kernel(k_top, qi_ref, wi_ref, q_ref, kidx_ref, ckv_ref, ckvt_ref, wuk_ref, wuv_ref, o_ref,
                 sc_sc, sc16_sc, qit_sc, qabst_sc, acct_sc, m_sc, l_sc, alpha_sc, p_sc, s0_sc, thr_sc, excess_sc):
    qb = pl.program_id(1)
    n_chunks = sc_sc.shape[0]
    nch = qb // (KEY_CHUNK // ROW_TILE) + 1
    q_pos = qb * ROW_TILE + lax.broadcasted_iota(jnp.int32, (1, ROW_TILE), 1)
    k_iota = lax.broadcasted_iota(jnp.int32, (KEY_CHUNK, 1), 0)

    qit = qi_ref[0].astype(jnp.float32).T
    for h in range(IDX_HEADS):
        qit_sc[:, h * ROW_TILE:(h + 1) * ROW_TILE] = qit[h * IDX_DIM:(h + 1) * IDX_DIM, :].astype(jnp.bfloat16)
    wit = wi_ref[0].T
    wi_row = jnp.concatenate([wit[h:h + 1, :] for h in range(IDX_HEADS)], axis=1)
    qt = q_ref[0].astype(jnp.float32).T
    scale = HEAD_DIM ** -0.5 * math.log2(math.e)
    for h in range(N_HEADS):
        qa = jnp.dot(wuk_ref[h], qt[h * HEAD_DIM:(h + 1) * HEAD_DIM, :].astype(jnp.bfloat16),
                     preferred_element_type=jnp.float32)
        qabst_sc[:, h * ROW_TILE:(h + 1) * ROW_TILE] = (qa * scale).astype(jnp.bfloat16)

    def score_chunk(c, carry):
        ks = kidx_ref[0, c][:, 0:IDX_DIM]
        sc = None
        for j in range(N_COLS // COL_TILE):
            cols = slice(j * COL_TILE, (j + 1) * COL_TILE)
            lg = jnp.dot(ks, qit_sc[:, cols], preferred_element_type=jnp.float32)
            w = jnp.maximum(lg, 0.0) * wi_row[:, cols]
            part = w[:, :ROW_TILE] + w[:, ROW_TILE:]
            sc = part if sc is None else sc + part
        k_pos = c * KEY_CHUNK + k_iota
        visible = jnp.logical_and(k_pos <= q_pos, k_pos >= PADF)
        sc = jnp.where(visible, sc, -jnp.inf)
        sc_sc[c] = sc
        sc16_sc[c] = sc.astype(jnp.bfloat16)
        return carry

    lax.fori_loop(0, nch, score_chunk, 0)

    def count(pred):
        def body(c, acc):
            hit = jnp.where(pred(sc_sc[c], c), 1, 0)
            return acc + jnp.sum(hit.reshape(KEY_CHUNK // 8, 8, ROW_TILE), axis=0)
        acc = lax.fori_loop(0, nch, body, jnp.zeros((8, ROW_TILE), jnp.int32))
        return jnp.sum(acc, axis=0, keepdims=True)

    def threshold_search(n):
        def count_ge(cand):
            acc = jnp.zeros((8, ROW_TILE), jnp.int32)
            for c in range(n):
                hit = jnp.where(sc_sc[c] >= cand, 1, 0)
                acc = acc + jnp.sum(hit.reshape(KEY_CHUNK // 8, 8, ROW_TILE), axis=0)
            return jnp.sum(acc, axis=0, keepdims=True)

        def count16_ge(cand):
            one = jnp.ones((), jnp.bfloat16)
            zero = jnp.zeros((), jnp.bfloat16)
            acc = jnp.zeros((16, ROW_TILE), jnp.float32)
            for c in range(n):
                hit = jnp.where(sc16_sc[c] >= cand, one, zero).reshape(KEY_CHUNK // 16, 16, ROW_TILE)
                parts = [hit[i] for i in range(KEY_CHUNK // 16)]
                while len(parts) > 1:
                    parts = [parts[i] + parts[i + 1] for i in range(0, len(parts), 2)]
                acc = acc + parts[0].astype(jnp.float32)
            return jnp.sum(acc, axis=0, keepdims=True)

        def coarse_pass(i, base):
            cand = base + lax.shift_left(jnp.int32(1), 15 - i)
            raw = cand ^ (lax.shift_right_arithmetic(cand, 15) & 0x7FFF)
            cand_f = pltpu.bitcast(lax.shift_left(raw, 16), jnp.float32)
            cnt = count16_ge(cand_f.astype(jnp.bfloat16))
            return jnp.where(cnt >= k_top, cand, base)

        v16 = lax.fori_loop(0, 16, coarse_pass, jnp.full((1, ROW_TILE), -(2 ** 15), jnp.int32))
        few = v16 == -(2 ** 15)

        lo_k = lax.shift_left(v16 - 1, 16)
        span = 3 * 2 ** 16

        def fine_pass(i, state):
            off, n_ge = state
            cand_off = off + lax.shift_left(jnp.int32(1), 17 - i)
            cnt = count_ge(_ordered_bits_to_f32(lo_k + cand_off))
            keep = jnp.logical_and(cnt >= k_top, cand_off < span)
            return jnp.where(keep, cand_off, off), jnp.where(keep, cnt, n_ge)

        off, n_ge = lax.fori_loop(0, 18, fine_pass, (jnp.zeros((1, ROW_TILE), jnp.int32),
                                                     jnp.zeros((1, ROW_TILE), jnp.int32)))
        thr_sc[...] = jnp.where(few, LOWEST, _ordered_bits_to_f32(lo_k + off))
        excess_sc[...] = jnp.where(jnp.logical_and(n_ge > k_top, jnp.logical_not(few)), 1, 0)

    for n in range(1, n_chunks + 1):
        pl.when(nch == n)(functools.partial(threshold_search, n))
    thr = thr_sc[...]
    excess = excess_sc[...] > 0


    @pl.when(jnp.max(jnp.where(excess, 1, 0)) > 0)
    def _():
        need = k_top - count(lambda s, c: s > thr)

        def idx_pass(i, p):
            cand = p + lax.shift_left(jnp.int32(1), 13 - i)
            cnt = count(lambda s, c: jnp.logical_and(s == thr, c * KEY_CHUNK + k_iota < cand))
            return jnp.where(cnt < need, cand, p)

        last = lax.fori_loop(0, 14, idx_pass, jnp.zeros((1, ROW_TILE), jnp.int32))

        def demote(c, carry):
            s = sc_sc[c]
            drop = jnp.logical_and(jnp.logical_and(excess, s == thr), c * KEY_CHUNK + k_iota > last)
            sc_sc[c] = jnp.where(drop, -jnp.inf, s)
            return carry

        lax.fori_loop(0, nch, demote, 0)

    n_tiles = N_COLS // COL_TILE
    last = slice((n_tiles - 1) * COL_TILE, n_tiles * COL_TILE)
    m_sc[...] = jnp.full_like(m_sc, NEG_INF)
    l_sc[...] = jnp.zeros_like(l_sc)
    acct_sc[...] = jnp.zeros_like(acct_sc)
    alpha_sc[:, last] = jnp.ones((1, COL_TILE), jnp.float32)
    p_sc[:, last] = jnp.zeros((KEY_CHUNK, COL_TILE), jnp.bfloat16)

    def logits_tile(kc, j):
        cols = slice(j * COL_TILE, (j + 1) * COL_TILE)
        return jnp.dot(kc, qabst_sc[:, cols], preferred_element_type=jnp.float32)

    def softmax_tile(s, sel, j):
        cols = slice(j * COL_TILE, (j + 1) * COL_TILE)
        s = jnp.concatenate([jnp.where(sel, s[:, :ROW_TILE], NEG_INF),
                             jnp.where(sel, s[:, ROW_TILE:], NEG_INF)], axis=1)
        m_prev = m_sc[:, cols]
        m_new = jnp.maximum(m_prev, jnp.max(s, axis=0, keepdims=True))
        p = jnp.exp2(s - m_new)
        alpha = jnp.exp2(m_prev - m_new)
        l_sc[:, cols] = alpha * l_sc[:, cols] + jnp.sum(p, axis=0, keepdims=True)
        m_sc[:, cols] = m_new
        alpha_sc[:, cols] = alpha
        p_sc[:, cols] = p.astype(jnp.bfloat16)

    def value_tile(kct, j):
        cols = slice(j * COL_TILE, (j + 1) * COL_TILE)
        acct_sc[:, cols] = alpha_sc[:, cols] * acct_sc[:, cols] + jnp.dot(
            kct, p_sc[:, cols], preferred_element_type=jnp.float32)

    def attn_chunk(c, carry):
        kc = ckv_ref[0, c]
        kct = ckvt_ref[0, c]
        sel = sc_sc[c] >= thr
        softmax_tile(s0_sc[...], sel, 0)
        value_tile(ckvt_ref[0, jnp.maximum(c - 1, 0)], n_tiles - 1)
        for j in range(1, n_tiles):
            softmax_tile(logits_tile(kc, j), sel, j)
            value_tile(kct, j - 1)
        s0_sc[...] = logits_tile(ckv_ref[0, jnp.minimum(c + 1, n_chunks - 1)], 0)
        return carry

    s0_sc[...] = logits_tile(ckv_ref[0, 0], 0)
    lax.fori_loop(0, nch, attn_chunk, 0)
    value_tile(ckvt_ref[0, nch - 1], n_tiles - 1)

    o_lat_t = acct_sc[...] / l_sc[...]
    for h in range(N_HEADS):
        o_lat = o_lat_t[:, h * ROW_TILE:(h + 1) * ROW_TILE].T.astype(jnp.bfloat16)
        oh = jnp.dot(o_lat, wuv_ref[h], preferred_element_type=jnp.float32)
        o_ref[0, :, h * HEAD_DIM:(h + 1) * HEAD_DIM] = oh.astype(jnp.bfloat16)


def _sparse_attention(qidx, widx, q, kidx_c, ckv_c, ckvt_c, wuk, wuv, k_top):
    B, tp, _ = q.shape
    nch = kidx_c.shape[1]
    return pl.pallas_call(
        functools.partial(_attn_kernel, k_top),
        grid=(B, tp // ROW_TILE),
        in_specs=[
            pl.BlockSpec((1, ROW_TILE, P_QIDX), lambda bi, t: (bi, t, 0)),
            pl.BlockSpec((1, ROW_TILE, P_WIDX), lambda bi, t: (bi, t, 0)),
            pl.BlockSpec((1, ROW_TILE, P_Q), lambda bi, t: (bi, t, 0)),
            pl.BlockSpec((1, nch, KEY_CHUNK, P_KIDX), lambda bi, t: (bi, 0, 0, 0)),
            pl.BlockSpec((1, nch, KEY_CHUNK, P_CKV), lambda bi, t: (bi, 0, 0, 0)),
            pl.BlockSpec((1, nch, P_CKV, KEY_CHUNK), lambda bi, t: (bi, 0, 0, 0)),
            _resident((N_HEADS, KV_RANK, HEAD_DIM)),
            _resident((N_HEADS, KV_RANK, HEAD_DIM)),
        ],
        out_specs=pl.BlockSpec((1, ROW_TILE, P_Q), lambda bi, t: (bi, t, 0)),
        out_shape=jax.ShapeDtypeStruct((B, tp, P_Q), jnp.bfloat16),
        scratch_shapes=[
            pltpu.VMEM((nch, KEY_CHUNK, ROW_TILE), jnp.float32),
            pltpu.VMEM((nch, KEY_CHUNK, ROW_TILE), jnp.bfloat16),
            pltpu.VMEM((IDX_DIM, N_COLS), jnp.bfloat16),
            pltpu.VMEM((KV_RANK, N_COLS), jnp.bfloat16),
            pltpu.VMEM((KV_RANK, N_COLS), jnp.float32),
            pltpu.VMEM((1, N_COLS), jnp.float32),
            pltpu.VMEM((1, N_COLS), jnp.float32),
            pltpu.VMEM((1, N_COLS), jnp.float32),
            pltpu.VMEM((KEY_CHUNK, N_COLS), jnp.bfloat16),
            pltpu.VMEM((KEY_CHUNK, COL_TILE), jnp.float32),
            pltpu.VMEM((1, ROW_TILE), jnp.float32),
            pltpu.VMEM((1, ROW_TILE), jnp.int32),
        ],
        compiler_params=_cparams(("parallel", "arbitrary")),
        name="sparse_attn",
    )(qidx, widx, q, kidx_c, ckv_c, ckvt_c, wuk, wuv)


def _merge_kernel(alpha, x_ref, hg_ref, o_ref, ga_ref, gb_ref, wa_ref, wb_ref, wo_ref, g_ref, b_ref, y_ref):
    ya = jnp.dot(hg_ref[...], wa_ref[...], preferred_element_type=jnp.float32)
    yb = jnp.dot(o_ref[...], wb_ref[...], preferred_element_type=jnp.float32)
    mixed = _sigmoid(ga_ref[...]) * ya + _sigmoid(gb_ref[...]) * yb
    z = jnp.dot(mixed.astype(jnp.bfloat16), wo_ref[...], preferred_element_type=jnp.float32)
    y_ref[...] = _layer_norm(alpha * x_ref[...] + z, g_ref[...], b_ref[...])


def _merge(alpha, x2d, hg, o, ga, gb, w_a, w_b, w_o, g, b, tm):
    n = x2d.shape[0]

    def rows(w):
        return pl.BlockSpec((tm, w), lambda i: (i, 0))

    return pl.pallas_call(
        functools.partial(_merge_kernel, alpha),
        grid=(n // tm,),
        in_specs=[rows(D_MODEL), rows(D_RNN), rows(P_Q), rows(D_MODEL), rows(D_MODEL),
                  _resident((D_RNN, D_MODEL)), _resident((P_Q, D_MODEL)), _resident((D_MODEL, D_MODEL)),
                  _resident((1, D_MODEL)), _resident((1, D_MODEL))],
        out_specs=rows(D_MODEL),
        out_shape=jax.ShapeDtypeStruct((n, D_MODEL), jnp.float32),
        compiler_params=_cparams(("parallel",)),
        name="merge",
    )(x2d, hg, o, ga, gb, w_a, w_b, w_o, g, b)


def _mlp_kernel(alpha, x_ref, wu_ref, bu_ref, wd_ref, bd_ref, g_ref, b_ref, y_ref):
    x = x_ref[...]
    h = jnp.dot(x.astype(jnp.bfloat16), wu_ref[...], preferred_element_type=jnp.float32) + bu_ref[...]
    h = jnp.maximum(h, 0.0)
    h = (h * h).astype(jnp.bfloat16)
    z = jnp.dot(h, wd_ref[...], preferred_element_type=jnp.float32) + bd_ref[...]
    y_ref[...] = _layer_norm(alpha * x + z, g_ref[...], b_ref[...])


def _mlp(alpha, x2d, w_up, b_up, w_down, b_down, g, b, tm):
    n = x2d.shape[0]
    return pl.pallas_call(
        functools.partial(_mlp_kernel, alpha),
        grid=(n // tm,),
        in_specs=[pl.BlockSpec((tm, D_MODEL), lambda i: (i, 0)),
                  _resident((D_MODEL, D_FF)), _resident((1, D_FF)),
                  _resident((D_FF, D_MODEL)), _resident((1, D_MODEL)),
                  _resident((1, D_MODEL)), _resident((1, D_MODEL))],
        out_specs=pl.BlockSpec((tm, D_MODEL), lambda i: (i, 0)),
        out_shape=jax.ShapeDtypeStruct((n, D_MODEL), jnp.float32),
        compiler_params=_cparams(("parallel",)),
        name="mlp",
    )(x2d, w_up, b_up, w_down, b_down, g, b)


def _pad_in_proj(w_in):
    splits = (D_RNN, D_RNN, N_HEADS * HEAD_DIM, KV_RANK, IDX_HEADS * IDX_DIM, IDX_DIM, IDX_HEADS,
              D_MODEL, D_MODEL)
    padded = (P_LRU, P_G, P_Q, P_CKV, P_QIDX, P_KIDX, P_WIDX, P_GA, P_GB)
    parts, off = [], 0
    for w, pw in zip(splits, padded):
        part = w_in[:, off:off + w]
        if pw > w:
            part = jnp.pad(part, ((0, 0), (0, pw - w)))
        parts.append(part)
        off += w
    return jnp.concatenate(parts, axis=1).astype(jnp.bfloat16)


def kernel(x, meta_tokens, ln_in_g, ln_in_b, w_in, conv_w, conv_b, w_rg_a, b_rg_a, w_rg_x, b_rg_x, lru_lambda,
           kv_norm_g, w_uk, w_uv, w_branch_a, w_branch_b, w_out, ln1_g, ln1_b, w_up, b_up, w_down, b_down,
           ln2_g, ln2_b):
    B, S, _ = x.shape
    depth = w_in.shape[0]
    assert S % ROW_TILE == 0
    T = S + N_META
    k_top = min(TOPK_MAX, T // 4)
    alpha = (2.0 * depth) ** 0.25
    n_blk = S // ROW_TILE + 1
    tp = n_blk * ROW_TILE
    tk = -(-tp // KEY_CHUNK) * KEY_CHUNK
    n = B * tp
    tm_proj, tm = 320, 640
    assert n % tm_proj == 0 and n % tm == 0
    bf16 = jnp.bfloat16

    def vec(a):
        return a.reshape(1, -1)

    meta_pad = jnp.pad(meta_tokens.astype(x.dtype), ((PADF, 0), (0, 0)))
    h = _ln_in(x, meta_pad, vec(ln_in_g), vec(ln_in_b), n_blk).reshape(n, D_MODEL)

    for l in range(depth):
        lru, g_lru, q, ckv, qidx, kidx, widx, ga, gb = _proj(h, _pad_in_proj(w_in[l]), vec(kv_norm_g[l]), tm_proj)

        hg = _recurrent(lru.reshape(B, tp, D_RNN), g_lru.reshape(B, tp, D_RNN), conv_w[l], vec(conv_b[l]),
                        w_rg_a[l].astype(bf16), vec(b_rg_a[l]), w_rg_x[l].astype(bf16), vec(b_rg_x[l]),
                        vec(lru_lambda[l]))

        key_pad = ((0, 0), (0, tk - tp), (0, 0))
        kidx_c = jnp.pad(kidx.reshape(B, tp, P_KIDX), key_pad).reshape(B, tk // KEY_CHUNK, KEY_CHUNK, P_KIDX)
        ckv_c = jnp.pad(ckv.reshape(B, tp, P_CKV), key_pad).reshape(B, tk // KEY_CHUNK, KEY_CHUNK, P_CKV)
        o = _sparse_attention(qidx.reshape(B, tp, P_QIDX), widx.reshape(B, tp, P_WIDX), q.reshape(B, tp, P_Q),
                              kidx_c, ckv_c, jnp.swapaxes(ckv_c, 2, 3),
                              w_uk[l].astype(bf16), w_uv[l].astype(bf16), k_top)

        h = _merge(alpha, h, hg.reshape(n, D_RNN), o.reshape(n, P_Q), ga, gb, w_branch_a[l].astype(bf16),
                   w_branch_b[l].astype(bf16), w_out[l].astype(bf16), vec(ln1_g[l]), vec(ln1_b[l]), tm)
        h = _mlp(alpha, h, w_up[l].astype(bf16), vec(b_up[l]), w_down[l].astype(bf16), vec(b_down[l]),
                 vec(ln2_g[l]), vec(ln2_b[l]), tm)

    return h.reshape(B, tp, D_MODEL)[:, ROW_TILE:, :]
```

```python
import functools
import math

import jax
import jax.numpy as jnp
from jax import lax
from jax.experimental import pallas as pl
from jax.experimental.pallas import tpu as pltpu

D_MODEL = 1024
N_META = 16
D_RNN = 1280
LRU_BLOCKS = 10
LRU_BLOCK = D_RNN // LRU_BLOCKS
CONV_WIDTH = 4
LRU_C = 8.0
N_HEADS = 8
HEAD_DIM = 128
KV_RANK = 256
IDX_HEADS = 8
IDX_DIM = 64
TOPK_MAX = 256
D_FF = 4 * D_MODEL
LN_EPS = 1e-5
NEG_INF = -1e30

LANES = 128
ROW_TILE = 128
PADF = ROW_TILE - N_META
KEY_CHUNK = 512
VMEM_LIMIT = 56 * 1024 * 1024

P_LRU, P_G, P_Q, P_CKV, P_QIDX, P_KIDX, P_WIDX, P_GA, P_GB = (
    D_RNN, D_RNN, N_HEADS * HEAD_DIM, KV_RANK, IDX_HEADS * IDX_DIM, LANES, LANES, D_MODEL, D_MODEL)
P_OFFS = []
_acc = 0
for _w in (P_LRU, P_G, P_Q, P_CKV, P_QIDX, P_KIDX, P_WIDX, P_GA, P_GB):
    P_OFFS.append(_acc)
    _acc += _w
P_TOTAL = _acc


def _cparams(sem):
    return pltpu.CompilerParams(dimension_semantics=sem, vmem_limit_bytes=VMEM_LIMIT)


def _resident(shape):
    nd = len(shape)
    return pl.BlockSpec(shape, lambda *_: (0,) * nd, pipeline_mode=pl.Buffered(1))


def _layer_norm(x, g, b):
    mu = jnp.mean(x, axis=-1, keepdims=True)
    d = x - mu
    var = jnp.mean(d * d, axis=-1, keepdims=True)
    return d * lax.rsqrt(var + LN_EPS) * g + b


def _sigmoid(x):
    return 1.0 / (1.0 + jnp.exp(-x))


def _ln_in_kernel(x_ref, meta_ref, g_ref, b_ref, o_ref):
    t = pl.program_id(1)

    @pl.when(t == 0)
    def _():
        o_ref[0] = _layer_norm(meta_ref[...], g_ref[...], b_ref[...])

    @pl.when(t > 0)
    def _():
        o_ref[0] = _layer_norm(x_ref[0], g_ref[...], b_ref[...])


def _ln_in(x, meta_pad, g, b, n_blk):
    B = x.shape[0]
    return pl.pallas_call(
        _ln_in_kernel,
        grid=(B, n_blk),
        in_specs=[
            pl.BlockSpec((1, ROW_TILE, D_MODEL), lambda bi, t: (bi, jnp.maximum(t - 1, 0), 0)),
            pl.BlockSpec((ROW_TILE, D_MODEL), lambda bi, t: (0, 0)),
            pl.BlockSpec((1, D_MODEL), lambda bi, t: (0, 0)),
            pl.BlockSpec((1, D_MODEL), lambda bi, t: (0, 0)),
        ],
        out_specs=pl.BlockSpec((1, ROW_TILE, D_MODEL), lambda bi, t: (bi, t, 0)),
        out_shape=jax.ShapeDtypeStruct((B, n_blk * ROW_TILE, D_MODEL), jnp.float32),
        compiler_params=_cparams(("parallel", "parallel")),
        name="ln_in",
    )(x, meta_pad, g, b)


def _proj_kernel(x_ref, w_ref, kvg_ref, lru_ref, g_ref, q_ref, ckv_ref, qidx_ref, kidx_ref, widx_ref,
                 ga_ref, gb_ref):
    xb = x_ref[...].astype(jnp.bfloat16)

    def seg(i, width):
        return jnp.dot(xb, w_ref[:, P_OFFS[i]:P_OFFS[i] + width], preferred_element_type=jnp.float32)

    lru_ref[...] = seg(0, P_LRU)
    g_ref[...] = seg(1, P_G)
    q_ref[...] = seg(2, P_Q).astype(jnp.bfloat16)

    c = seg(3, P_CKV)
    ms = jnp.mean(c * c, axis=-1, keepdims=True)
    ckv_ref[...] = (c * lax.rsqrt(ms + LN_EPS) * kvg_ref[...]).astype(jnp.bfloat16)

    qidx_ref[...] = seg(4, P_QIDX).astype(jnp.bfloat16)

    k = seg(5, P_KIDX)
    lane = lax.broadcasted_iota(jnp.int32, k.shape, 1)
    mu = jnp.sum(k, axis=-1, keepdims=True) * (1.0 / IDX_DIM)
    d = jnp.where(lane < IDX_DIM, k - mu, 0.0)
    var = jnp.sum(d * d, axis=-1, keepdims=True) * (1.0 / IDX_DIM)
    kidx_ref[...] = (d * lax.rsqrt(var + LN_EPS)).astype(jnp.bfloat16)

    widx_ref[...] = seg(6, P_WIDX) * (IDX_HEADS ** -0.5 * IDX_DIM ** -0.5)
    ga_ref[...] = seg(7, P_GA)
    gb_ref[...] = seg(8, P_GB)


def _proj(h2d, w_in_p, kv_g, tm):
    n = h2d.shape[0]
    f32, bf16 = jnp.float32, jnp.bfloat16
    widths = (P_LRU, P_G, P_Q, P_CKV, P_QIDX, P_KIDX, P_WIDX, P_GA, P_GB)
    dtypes = (f32, f32, bf16, bf16, bf16, bf16, f32, f32, f32)
    return pl.pallas_call(
        _proj_kernel,
        grid=(n // tm,),
        in_specs=[
            pl.BlockSpec((tm, D_MODEL), lambda i: (i, 0)),
            _resident((D_MODEL, P_TOTAL)),
            _resident((1, KV_RANK)),
        ],
        out_specs=[pl.BlockSpec((tm, w), lambda i: (i, 0)) for w in widths],
        out_shape=[jax.ShapeDtypeStruct((n, w), dt) for w, dt in zip(widths, dtypes)],
        compiler_params=_cparams(("parallel",)),
        name="proj",
    )(h2d, w_in_p, kv_g)


def _rec_kernel(x_ref, g_ref, cw_ref, cb_ref, wa_ref, ba_ref, wx_ref, bx_ref, lam_ref, o_ref,
                xe_sc, a_sc, u_sc, h_sc):
    tb = pl.program_id(1)

    @pl.when(tb == 0)
    def _():
        xe_sc[0:8, :] = jnp.zeros((8, D_RNN), jnp.float32)
        h_sc[...] = jnp.zeros_like(h_sc)

    row = lax.broadcasted_iota(jnp.int32, (ROW_TILE, 1), 0)
    real = jnp.logical_or(tb > 0, row >= PADF)
    x = jnp.where(real, x_ref[0], 0.0)
    xe_sc[8:8 + ROW_TILE, :] = x
    cw = cw_ref[...]
    xc = (cw[3:4] * x + cw[2:3] * xe_sc[7:7 + ROW_TILE, :] + cw[1:2] * xe_sc[6:6 + ROW_TILE, :]
          + cw[0:1] * xe_sc[5:5 + ROW_TILE, :] + cb_ref[...])
    xe_sc[0:8, :] = xe_sc[ROW_TILE:ROW_TILE + 8, :]

    xcb = xc.astype(jnp.bfloat16)
    lam = lam_ref[...]
    z = -lam
    softplus = jnp.maximum(z, 0.0) + jnp.log1p(jnp.exp(-jnp.abs(z)))
    for n in range(LRU_BLOCKS):
        sl = slice(n * LRU_BLOCK, (n + 1) * LRU_BLOCK)
        xn = xcb[:, sl]
        r = _sigmoid(jnp.dot(xn, wa_ref[n], preferred_element_type=jnp.float32) + ba_ref[:, sl])
        i = _sigmoid(jnp.dot(xn, wx_ref[n], preferred_element_type=jnp.float32) + bx_ref[:, sl])
        log_a = -LRU_C * r * softplus[:, sl]
        a = jnp.exp(log_a)
        a_sc[:, sl] = a
        u = jnp.sqrt(-jnp.tanh(log_a) * (a * a + 1.0)) * (i * xc[:, sl])
        u_sc[:, sl] = jnp.where(real, u, 0.0)

    def step(t, h):
        h = a_sc[pl.ds(t, 1), :] * h + u_sc[pl.ds(t, 1), :]
        u_sc[pl.ds(t, 1), :] = h
        return h

    h_sc[...] = lax.fori_loop(0, ROW_TILE, step, h_sc[...], unroll=8)

    g = g_ref[0]
    gelu = 0.5 * g * (1.0 + jnp.tanh(math.sqrt(2.0 / math.pi) * (g + 0.044715 * (g * g * g))))
    o_ref[0] = (u_sc[...] * gelu).astype(jnp.bfloat16)


def _recurrent(x_lru, g_lru, conv_w, conv_b, w_a, b_a, w_x, b_x, lam):
    B, tp, _ = x_lru.shape
    blk = pl.BlockSpec((1, ROW_TILE, D_RNN), lambda bi, t: (bi, t, 0))
    vec = pl.BlockSpec((1, D_RNN), lambda bi, t: (0, 0))
    wblk = pl.BlockSpec((LRU_BLOCKS, LRU_BLOCK, LRU_BLOCK), lambda bi, t: (0, 0, 0))
    return pl.pallas_call(
        _rec_kernel,
        grid=(B, tp // ROW_TILE),
        in_specs=[blk, blk, pl.BlockSpec((CONV_WIDTH, D_RNN), lambda bi, t: (0, 0)), vec,
                  wblk, vec, wblk, vec, vec],
        out_specs=blk,
        out_shape=jax.ShapeDtypeStruct((B, tp, D_RNN), jnp.bfloat16),
        scratch_shapes=[
            pltpu.VMEM((ROW_TILE + 8, D_RNN), jnp.float32),
            pltpu.VMEM((ROW_TILE, D_RNN), jnp.float32),
            pltpu.VMEM((ROW_TILE, D_RNN), jnp.float32),
            pltpu.VMEM((1, D_RNN), jnp.float32),
        ],
        compiler_params=_cparams(("parallel", "arbitrary")),
        name="recurrent",
    )(x_lru, g_lru, conv_w, conv_b, w_a, b_a, w_x, b_x, lam)


N_COLS = N_HEADS * ROW_TILE
COL_TILE = 2 * ROW_TILE
LOWEST = -3.0e38


def _ordered_bits_to_f32(k):
    return pltpu.bitcast(k ^ (lax.shift_right_arithmetic(k, 31) & 0x7FFFFFFF), jnp.float32)


def _attn_kernel(k_top, n_blocks, qi_ref, wi_ref, q_ref, kidx_ref, ckv_ref, ckvt_ref, wuk_ref, wuv_ref, tril_ref,
                 o_ref, sc_sc, sc16_sc, qit_sc, qabst_sc, acct_sc, m_sc, l_sc, alpha_sc, p_sc, s0_sc, thr_sc,
                 excess_sc):
    t = pl.program_id(1)
    n_chunks = sc_sc.shape[1]
    chunk_blocks = KEY_CHUNK // ROW_TILE
    blk_s = jnp.minimum(t, n_blocks - 1)
    nch_s = blk_s // chunk_blocks + 1
    nch_a = jnp.where(t > 0, (t - 1) // chunk_blocks + 1, 0)
    slot_s = lax.rem(t, 2)
    slot_a = 1 - slot_s
    q_pos = blk_s * ROW_TILE + lax.broadcasted_iota(jnp.int32, (1, ROW_TILE), 1)
    k_iota = lax.broadcasted_iota(jnp.int32, (KEY_CHUNK, 1), 0)

    qit = qi_ref[0].astype(jnp.float32).T
    for h in range(IDX_HEADS):
        qit_sc[:, h * ROW_TILE:(h + 1) * ROW_TILE] = qit[h * IDX_DIM:(h + 1) * IDX_DIM, :].astype(jnp.bfloat16)
    wit = wi_ref[0].T
    wi_row = jnp.concatenate([wit[h:h + 1, :] for h in range(IDX_HEADS)], axis=1)
    qt = q_ref[0].astype(jnp.float32).T
    scale = HEAD_DIM ** -0.5 * math.log2(math.e)
    for h in range(N_HEADS):
        qa = jnp.dot(wuk_ref[h], qt[h * HEAD_DIM:(h + 1) * HEAD_DIM, :].astype(jnp.bfloat16),
                     preferred_element_type=jnp.float32)
        qabst_sc[:, h * ROW_TILE:(h + 1) * ROW_TILE] = (qa * scale).astype(jnp.bfloat16)

    def score_chunk(c):
        ks = kidx_ref[0, c][:, 0:IDX_DIM]
        sc = None
        for j in range(N_COLS // COL_TILE):
            cols = slice(j * COL_TILE, (j + 1) * COL_TILE)
            lg = jnp.dot(ks, qit_sc[:, cols], preferred_element_type=jnp.float32)
            w = jnp.maximum(lg, 0.0) * wi_row[:, cols]
            part = w[:, :ROW_TILE] + w[:, ROW_TILE:]
            sc = part if sc is None else sc + part
        k_pos = c * KEY_CHUNK + k_iota
        visible = jnp.logical_and(k_pos <= q_pos, k_pos >= PADF)
        sc = jnp.where(visible, sc, -jnp.inf)
        sc_sc[slot_s, c] = sc
        sc16_sc[slot_s, c] = sc.astype(jnp.bfloat16)

    @pl.when(t == 0)
    def _():
        thr_sc[...] = jnp.full_like(thr_sc, LOWEST)

    thr_a = thr_sc[...]
    n_tiles = N_COLS // COL_TILE
    last = slice((n_tiles - 1) * COL_TILE, n_tiles * COL_TILE)
    m_sc[...] = jnp.full_like(m_sc, NEG_INF)
    l_sc[...] = jnp.zeros_like(l_sc)
    acct_sc[...] = jnp.zeros_like(acct_sc)
    alpha_sc[:, last] = jnp.ones((1, COL_TILE), jnp.float32)
    p_sc[:, last] = jnp.zeros((KEY_CHUNK, COL_TILE), jnp.bfloat16)

    def logits_tile(kc, j):
        cols = slice(j * COL_TILE, (j + 1) * COL_TILE)
        return jnp.dot(kc, qabst_sc[:, cols], preferred_element_type=jnp.float32)

    def softmax_tile(s, sel, j):
        cols = slice(j * COL_TILE, (j + 1) * COL_TILE)
        s = jnp.concatenate([jnp.where(sel, s[:, :ROW_TILE], NEG_INF),
                             jnp.where(sel, s[:, ROW_TILE:], NEG_INF)], axis=1)
        m_prev = m_sc[:, cols]
        m_new = jnp.maximum(m_prev, jnp.max(s, axis=0, keepdims=True))
        p = jnp.exp2(s - m_new)
        alpha = jnp.exp2(m_prev - m_new)
        l_sc[:, cols] = alpha * l_sc[:, cols] + jnp.sum(p, axis=0, keepdims=True)
        m_sc[:, cols] = m_new
        alpha_sc[:, cols] = alpha
        p_sc[:, cols] = p.astype(jnp.bfloat16)

    def value_tile(kct, j):
        cols = slice(j * COL_TILE, (j + 1) * COL_TILE)
        acct_sc[:, cols] = alpha_sc[:, cols] * acct_sc[:, cols] + jnp.dot(
            kct, p_sc[:, cols], preferred_element_type=jnp.float32)

    def attn_chunk(c):
        kc = ckv_ref[0, c]
        kct = ckvt_ref[0, c]
        sel = sc_sc[slot_a, c] >= thr_a
        softmax_tile(s0_sc[...], sel, 0)
        value_tile(ckvt_ref[0, jnp.maximum(c - 1, 0)], n_tiles - 1)
        for j in range(1, n_tiles):
            softmax_tile(logits_tile(kc, j), sel, j)
            value_tile(kct, j - 1)
        s0_sc[...] = logits_tile(ckv_ref[0, jnp.minimum(c + 1, n_chunks - 1)], 0)

    def both(c, carry):
        attn_chunk(c)
        score_chunk(c)
        return carry

    def score_only(c, carry):
        score_chunk(c)
        return carry

    s0_sc[...] = logits_tile(ckv_ref[0, 0], 0)
    lax.fori_loop(0, nch_a, both, 0)
    lax.fori_loop(nch_a, nch_s, score_only, 0)

    @pl.when(t > 0)
    def _():
        value_tile(ckvt_ref[0, nch_a - 1], n_tiles - 1)
        o_lat_t = acct_sc[...] / l_sc[...]
        for h in range(N_HEADS):
            o_lat = o_lat_t[:, h * ROW_TILE:(h + 1) * ROW_TILE].T.astype(jnp.bfloat16)
            oh = jnp.dot(o_lat, wuv_ref[h], preferred_element_type=jnp.float32)
            o_ref[0, :, h * HEAD_DIM:(h + 1) * HEAD_DIM] = oh.astype(jnp.bfloat16)

    def threshold_search(n):
        def count_ge(cand):
            acc = jnp.zeros((8, ROW_TILE), jnp.int32)
            for c in range(n):
                hit = jnp.where(sc_sc[slot_s, c] >= cand, 1, 0)
                acc = acc + jnp.sum(hit.reshape(KEY_CHUNK // 8, 8, ROW_TILE), axis=0)
            return jnp.sum(acc, axis=0, keepdims=True)

        def count16_ge(cand):
            one = jnp.ones((), jnp.bfloat16)
            zero = jnp.zeros((), jnp.bfloat16)
            acc = jnp.zeros((16, ROW_TILE), jnp.float32)
            for c in range(n):
                hit = jnp.where(sc16_sc[slot_s, c] >= cand, one, zero).reshape(KEY_CHUNK // 16, 16, ROW_TILE)
                parts = [hit[i] for i in range(KEY_CHUNK // 16)]
                while len(parts) > 1:
                    parts = [parts[i] + parts[i + 1] for i in range(0, len(parts), 2)]
                acc = acc + parts[0].astype(jnp.float32)
            return jnp.sum(acc, axis=0, keepdims=True)

        def coarse_pass(i, base):
            cand = base + lax.shift_left(jnp.int32(1), 15 - i)
            raw = cand ^ (lax.shift_right_arithmetic(cand, 15) & 0x7FFF)
            cand_f = pltpu.bitcast(lax.shift_left(raw, 16), jnp.float32)
            cnt = count16_ge(cand_f.astype(jnp.bfloat16))
            return jnp.where(cnt >= k_top, cand, base)

        v16 = lax.fori_loop(0, 16, coarse_pass, jnp.full((1, ROW_TILE), -(2 ** 15), jnp.int32))
        few = v16 == -(2 ** 15)

        lo_k = lax.shift_left(v16 - 1, 16)
        span = 3 * 2 ** 16

        def fine_pass(i, state):
            off, n_ge = state
            cand_off = off + lax.shift_left(jnp.int32(1), 17 - i)
            cnt = count_ge(_ordered_bits_to_f32(lo_k + cand_off))
            keep = jnp.logical_and(cnt >= k_top, cand_off < span)
            return jnp.where(keep, cand_off, off), jnp.where(keep, cnt, n_ge)

        off, n_ge = lax.fori_loop(0, 18, fine_pass, (jnp.zeros((1, ROW_TILE), jnp.int32),
                                                     jnp.zeros((1, ROW_TILE), jnp.int32)))
        thr_sc[...] = jnp.where(few, LOWEST, _ordered_bits_to_f32(lo_k + off))
        excess_sc[...] = jnp.where(jnp.logical_and(n_ge > k_top, jnp.logical_not(few)), 1, 0)

    for n in range(1, n_chunks + 1):
        pl.when(jnp.logical_and(nch_s == n, t < n_blocks))(functools.partial(threshold_search, n))
    thr = thr_sc[...]
    excess = excess_sc[...] > 0

    @pl.when(jnp.logical_and(jnp.max(jnp.where(excess, 1, 0)) > 0, t < n_blocks))
    def _():
        def above(c, acc):
            hit = jnp.where(sc_sc[slot_s, c] > thr, 1, 0)
            return acc + jnp.sum(hit.reshape(KEY_CHUNK // 8, 8, ROW_TILE), axis=0)
        acc = lax.fori_loop(0, nch_s, above, jnp.zeros((8, ROW_TILE), jnp.int32))
        need = (k_top - jnp.sum(acc, axis=0, keepdims=True)).astype(jnp.float32)

        def demote(c, seen):
            s = sc_sc[slot_s, c]
            tied = s == thr
            rank = seen + jnp.dot(tril_ref[...], jnp.where(tied, 1.0, 0.0).astype(jnp.bfloat16),
                                  preferred_element_type=jnp.float32)
            drop = jnp.logical_and(jnp.logical_and(excess, tied), rank > need)
            sc_sc[slot_s, c] = jnp.where(drop, -jnp.inf, s)
            return rank[KEY_CHUNK - 1:KEY_CHUNK, :]

        lax.fori_loop(0, nch_s, demote, jnp.zeros((1, ROW_TILE), jnp.float32))


def _sparse_attention(qidx, widx, q, kidx_c, ckv_c, ckvt_c, wuk, wuv, k_top):
    B, tp, _ = q.shape
    nch = kidx_c.shape[1]
    nblk = tp // ROW_TILE

    def scored(bi, t):
        return (bi, jnp.minimum(t, nblk - 1), 0)

    def attended(bi, t):
        return (bi, jnp.maximum(t - 1, 0), 0)

    return pl.pallas_call(
        functools.partial(_attn_kernel, k_top, nblk),
        grid=(B, nblk + 1),
        in_specs=[
            pl.BlockSpec((1, ROW_TILE, P_QIDX), scored),
            pl.BlockSpec((1, ROW_TILE, P_WIDX), scored),
            pl.BlockSpec((1, ROW_TILE, P_Q), attended),
            pl.BlockSpec((1, nch, KEY_CHUNK, P_KIDX), lambda bi, t: (bi, 0, 0, 0)),
            pl.BlockSpec((1, nch, KEY_CHUNK, P_CKV), lambda bi, t: (bi, 0, 0, 0)),
            pl.BlockSpec((1, nch, P_CKV, KEY_CHUNK), lambda bi, t: (bi, 0, 0, 0)),
            _resident((N_HEADS, KV_RANK, HEAD_DIM)),
            _resident((N_HEADS, KV_RANK, HEAD_DIM)),
            _resident((KEY_CHUNK, KEY_CHUNK)),
        ],
        out_specs=pl.BlockSpec((1, ROW_TILE, P_Q), attended),
        out_shape=jax.ShapeDtypeStruct((B, tp, P_Q), jnp.bfloat16),
        scratch_shapes=[
            pltpu.VMEM((2, nch, KEY_CHUNK, ROW_TILE), jnp.float32),
            pltpu.VMEM((2, nch, KEY_CHUNK, ROW_TILE), jnp.bfloat16),
            pltpu.VMEM((IDX_DIM, N_COLS), jnp.bfloat16),
            pltpu.VMEM((KV_RANK, N_COLS), jnp.bfloat16),
            pltpu.VMEM((KV_RANK, N_COLS), jnp.float32),
            pltpu.VMEM((1, N_COLS), jnp.float32),
            pltpu.VMEM((1, N_COLS), jnp.float32),
            pltpu.VMEM((1, N_COLS), jnp.float32),
            pltpu.VMEM((KEY_CHUNK, N_COLS), jnp.bfloat16),
            pltpu.VMEM((KEY_CHUNK, COL_TILE), jnp.float32),
            pltpu.VMEM((1, ROW_TILE), jnp.float32),
            pltpu.VMEM((1, ROW_TILE), jnp.int32),
        ],
        compiler_params=_cparams(("parallel", "arbitrary")),
        name="sparse_attn",
    )(qidx, widx, q, kidx_c, ckv_c, ckvt_c, wuk, wuv, jnp.tril(jnp.ones((KEY_CHUNK, KEY_CHUNK), jnp.bfloat16)))


def _merge_kernel(alpha, x_ref, hg_ref, o_ref, ga_ref, gb_ref, wa_ref, wb_ref, wo_ref, g_ref, b_ref, y_ref):
    ya = jnp.dot(hg_ref[...], wa_ref[...], preferred_element_type=jnp.float32)
    yb = jnp.dot(o_ref[...], wb_ref[...], preferred_element_type=jnp.float32)
    mixed = _sigmoid(ga_ref[...]) * ya + _sigmoid(gb_ref[...]) * yb
    z = jnp.dot(mixed.astype(jnp.bfloat16), wo_ref[...], preferred_element_type=jnp.float32)
    y_ref[...] = _layer_norm(alpha * x_ref[...] + z, g_ref[...], b_ref[...])


def _merge(alpha, x2d, hg, o, ga, gb, w_a, w_b, w_o, g, b, tm):
    n = x2d.shape[0]

    def rows(w):
        return pl.BlockSpec((tm, w), lambda i: (i, 0))

    return pl.pallas_call(
        functools.partial(_merge_kernel, alpha),
        grid=(n // tm,),
        in_specs=[rows(D_MODEL), rows(D_RNN), rows(P_Q), rows(D_MODEL), rows(D_MODEL),
                  _resident((D_RNN, D_MODEL)), _resident((P_Q, D_MODEL)), _resident((D_MODEL, D_MODEL)),
                  _resident((1, D_MODEL)), _resident((1, D_MODEL))],
        out_specs=rows(D_MODEL),
        out_shape=jax.ShapeDtypeStruct((n, D_MODEL), jnp.float32),
        compiler_params=_cparams(("parallel",)),
        name="merge",
    )(x2d, hg, o, ga, gb, w_a, w_b, w_o, g, b)


def _mlp_kernel(alpha, x_ref, wu_ref, bu_ref, wd_ref, bd_ref, g_ref, b_ref, y_ref):
    x = x_ref[...]
    h = jnp.dot(x.astype(jnp.bfloat16), wu_ref[...], preferred_element_type=jnp.float32) + bu_ref[...]
    h = jnp.maximum(h, 0.0)
    h = (h * h).astype(jnp.bfloat16)
    z = jnp.dot(h, wd_ref[...], preferred_element_type=jnp.float32) + bd_ref[...]
    y_ref[...] = _layer_norm(alpha * x + z, g_ref[...], b_ref[...])


def _mlp(alpha, x2d, w_up, b_up, w_down, b_down, g, b, tm):
    n = x2d.shape[0]
    return pl.pallas_call(
        functools.partial(_mlp_kernel, alpha),
        grid=(n // tm,),
        in_specs=[pl.BlockSpec((tm, D_MODEL), lambda i: (i, 0)),
                  _resident((D_MODEL, D_FF)), _resident((1, D_FF)),
                  _resident((D_FF, D_MODEL)), _resident((1, D_MODEL)),
                  _resident((1, D_MODEL)), _resident((1, D_MODEL))],
        out_specs=pl.BlockSpec((tm, D_MODEL), lambda i: (i, 0)),
        out_shape=jax.ShapeDtypeStruct((n, D_MODEL), jnp.float32),
        compiler_params=_cparams(("parallel",)),
        name="mlp",
    )(x2d, w_up, b_up, w_down, b_down, g, b)


def _pad_in_proj(w_in):
    splits = (D_RNN, D_RNN, N_HEADS * HEAD_DIM, KV_RANK, IDX_HEADS * IDX_DIM, IDX_DIM, IDX_HEADS,
              D_MODEL, D_MODEL)
    padded = (P_LRU, P_G, P_Q, P_CKV, P_QIDX, P_KIDX, P_WIDX, P_GA, P_GB)
    parts, off = [], 0
    for w, pw in zip(splits, padded):
        part = w_in[:, off:off + w]
        if pw > w:
            part = jnp.pad(part, ((0, 0), (0, pw - w)))
        parts.append(part)
        off += w
    return jnp.concatenate(parts, axis=1).astype(jnp.bfloat16)


def kernel(x, meta_tokens, ln_in_g, ln_in_b, w_in, conv_w, conv_b, w_rg_a, b_rg_a, w_rg_x, b_rg_x, lru_lambda,
           kv_norm_g, w_uk, w_uv, w_branch_a, w_branch_b, w_out, ln1_g, ln1_b, w_up, b_up, w_down, b_down,
           ln2_g, ln2_b):
    B, S, _ = x.shape
    depth = w_in.shape[0]
    assert S % ROW_TILE == 0
    T = S + N_META
    k_top = min(TOPK_MAX, T // 4)
    alpha = (2.0 * depth) ** 0.25
    n_blk = S // ROW_TILE + 1
    tp = n_blk * ROW_TILE
    tk = -(-tp // KEY_CHUNK) * KEY_CHUNK
    n = B * tp
    tm_proj, tm = 320, 640
    assert n % tm_proj == 0 and n % tm == 0
    bf16 = jnp.bfloat16

    def vec(a):
        return a.reshape(1, -1)

    meta_pad = jnp.pad(meta_tokens.astype(x.dtype), ((PADF, 0), (0, 0)))
    h = _ln_in(x, meta_pad, vec(ln_in_g), vec(ln_in_b), n_blk).reshape(n, D_MODEL)

    for l in range(depth):
        lru, g_lru, q, ckv, qidx, kidx, widx, ga, gb = _proj(h, _pad_in_proj(w_in[l]), vec(kv_norm_g[l]), tm_proj)

        hg = _recurrent(lru.reshape(B, tp, D_RNN), g_lru.reshape(B, tp, D_RNN), conv_w[l], vec(conv_b[l]),
                        w_rg_a[l].astype(bf16), vec(b_rg_a[l]), w_rg_x[l].astype(bf16), vec(b_rg_x[l]),
                        vec(lru_lambda[l]))

        key_pad = ((0, 0), (0, tk - tp), (0, 0))
        kidx_c = jnp.pad(kidx.reshape(B, tp, P_KIDX), key_pad).reshape(B, tk // KEY_CHUNK, KEY_CHUNK, P_KIDX)
        ckv_c = jnp.pad(ckv.reshape(B, tp, P_CKV), key_pad).reshape(B, tk // KEY_CHUNK, KEY_CHUNK, P_CKV)
        o = _sparse_attention(qidx.reshape(B, tp, P_QIDX), widx.reshape(B, tp, P_WIDX), q.reshape(B, tp, P_Q),
                              kidx_c, ckv_c, jnp.swapaxes(ckv_c, 2, 3),
                              w_uk[l].astype(bf16), w_uv[l].astype(bf16), k_top)

        h = _merge(alpha, h, hg.reshape(n, D_RNN), o.reshape(n, P_Q), ga, gb, w_branch_a[l].astype(bf16),
                   w_branch_b[l].astype(bf16), w_out[l].astype(bf16), vec(ln1_g[l]), vec(ln1_b[l]), tm)
        h = _mlp(alpha, h, w_up[l].astype(bf16), vec(b_up[l]), w_down[l].astype(bf16), vec(b_down[l]),
                 vec(ln2_g[l]), vec(ln2_b[l]), tm)

    return h.reshape(B, tp, D_MODEL)[:, ROW_TILE:, :]
```

```python
import functools
import math

import jax
import jax.numpy as jnp
from jax import lax
from jax.experimental import pallas as pl
from jax.experimental.pallas import tpu as pltpu

D_MODEL = 1024
N_META = 16
D_RNN = 1280
LRU_BLOCKS = 10
LRU_BLOCK = D_RNN // LRU_BLOCKS
CONV_WIDTH = 4
LRU_C = 8.0
N_HEADS = 8
HEAD_DIM = 128
KV_RANK = 256
IDX_HEADS = 8
IDX_DIM = 64
TOPK_MAX = 256
D_FF = 4 * D_MODEL
LN_EPS = 1e-5
NEG_INF = -1e30

LANES = 128
ROW_TILE = 128
PADF = ROW_TILE - N_META
KEY_CHUNK = 512
VMEM_LIMIT = 56 * 1024 * 1024

P_LRU, P_G, P_Q, P_CKV, P_QIDX, P_KIDX, P_WIDX, P_GA, P_GB = (
    D_RNN, D_RNN, N_HEADS * HEAD_DIM, KV_RANK, IDX_HEADS * IDX_DIM, LANES, LANES, D_MODEL, D_MODEL)
P_OFFS = []
_acc = 0
for _w in (P_LRU, P_G, P_Q, P_CKV, P_QIDX, P_KIDX, P_WIDX, P_GA, P_GB):
    P_OFFS.append(_acc)
    _acc += _w
P_TOTAL = _acc


def _cparams(sem):
    return pltpu.CompilerParams(dimension_semantics=sem, vmem_limit_bytes=VMEM_LIMIT)


def _resident(shape):
    nd = len(shape)
    return pl.BlockSpec(shape, lambda *_: (0,) * nd, pipeline_mode=pl.Buffered(1))


def _layer_norm(x, g, b):
    mu = jnp.mean(x, axis=-1, keepdims=True)
    d = x - mu
    var = jnp.mean(d * d, axis=-1, keepdims=True)
    return d * lax.rsqrt(var + LN_EPS) * g + b


def _sigmoid(x):
    return 0.5 * jnp.tanh(0.5 * x) + 0.5


def _ln_in_kernel(x_ref, meta_ref, g_ref, b_ref, o_ref):
    t = pl.program_id(1)

    @pl.when(t == 0)
    def _():
        o_ref[0] = _layer_norm(meta_ref[...], g_ref[...], b_ref[...])

    @pl.when(t > 0)
    def _():
        o_ref[0] = _layer_norm(x_ref[0], g_ref[...], b_ref[...])


def _ln_in(x, meta_pad, g, b, n_blk):
    B = x.shape[0]
    return pl.pallas_call(
        _ln_in_kernel,
        grid=(B, n_blk),
        in_specs=[
            pl.BlockSpec((1, ROW_TILE, D_MODEL), lambda bi, t: (bi, jnp.maximum(t - 1, 0), 0)),
            pl.BlockSpec((ROW_TILE, D_MODEL), lambda bi, t: (0, 0)),
            pl.BlockSpec((1, D_MODEL), lambda bi, t: (0, 0)),
            pl.BlockSpec((1, D_MODEL), lambda bi, t: (0, 0)),
        ],
        out_specs=pl.BlockSpec((1, ROW_TILE, D_MODEL), lambda bi, t: (bi, t, 0)),
        out_shape=jax.ShapeDtypeStruct((B, n_blk * ROW_TILE, D_MODEL), jnp.float32),
        compiler_params=_cparams(("parallel", "parallel")),
        name="ln_in",
    )(x, meta_pad, g, b)


def _proj_kernel(x_ref, w_ref, kvg_ref, lru_ref, g_ref, q_ref, ckv_ref, qidx_ref, kidx_ref, widx_ref,
                 ga_ref, gb_ref):
    xb = x_ref[...].astype(jnp.bfloat16)

    def seg(i, width):
        return jnp.dot(xb, w_ref[:, P_OFFS[i]:P_OFFS[i] + width], preferred_element_type=jnp.float32)

    lru_ref[...] = seg(0, P_LRU)
    g = seg(1, P_G)
    gelu = 0.5 * g * (1.0 + jnp.tanh(math.sqrt(2.0 / math.pi) * (g + 0.044715 * (g * g * g))))
    g_ref[...] = gelu.astype(jnp.bfloat16)
    q_ref[...] = seg(2, P_Q).astype(jnp.bfloat16)

    c = seg(3, P_CKV)
    ms = jnp.mean(c * c, axis=-1, keepdims=True)
    ckv_ref[...] = (c * lax.rsqrt(ms + LN_EPS) * kvg_ref[...]).astype(jnp.bfloat16)

    qidx_ref[...] = seg(4, P_QIDX).astype(jnp.bfloat16)

    k = seg(5, P_KIDX)
    lane = lax.broadcasted_iota(jnp.int32, k.shape, 1)
    mu = jnp.sum(k, axis=-1, keepdims=True) * (1.0 / IDX_DIM)
    d = jnp.where(lane < IDX_DIM, k - mu, 0.0)
    var = jnp.sum(d * d, axis=-1, keepdims=True) * (1.0 / IDX_DIM)
    kidx_ref[...] = (d * lax.rsqrt(var + LN_EPS)).astype(jnp.bfloat16)

    widx_ref[...] = seg(6, P_WIDX) * (IDX_HEADS ** -0.5 * IDX_DIM ** -0.5)
    ga_ref[...] = _sigmoid(seg(7, P_GA))
    gb_ref[...] = _sigmoid(seg(8, P_GB))


def _proj(h2d, w_in_p, kv_g, tm):
    n = h2d.shape[0]
    f32, bf16 = jnp.float32, jnp.bfloat16
    widths = (P_LRU, P_G, P_Q, P_CKV, P_QIDX, P_KIDX, P_WIDX, P_GA, P_GB)
    dtypes = (f32, bf16, bf16, bf16, bf16, bf16, f32, f32, f32)
    return pl.pallas_call(
        _proj_kernel,
        grid=(n // tm,),
        in_specs=[
            pl.BlockSpec((tm, D_MODEL), lambda i: (i, 0)),
            _resident((D_MODEL, P_TOTAL)),
            _resident((1, KV_RANK)),
        ],
        out_specs=[pl.BlockSpec((tm, w), lambda i: (i, 0)) for w in widths],
        out_shape=[jax.ShapeDtypeStruct((n, w), dt) for w, dt in zip(widths, dtypes)],
        compiler_params=_cparams(("parallel",)),
        name="proj",
    )(h2d, w_in_p, kv_g)


def _rec_kernel(x_ref, g_ref, cw_ref, cb_ref, wa_ref, ba_ref, wx_ref, bx_ref, lam_ref, o_ref,
                xe_sc, a_sc, u_sc, h_sc):
    tb = pl.program_id(1)

    @pl.when(tb == 0)
    def _():
        xe_sc[0:8, :] = jnp.zeros((8, D_RNN), jnp.float32)
        h_sc[...] = jnp.zeros_like(h_sc)

    row = lax.broadcasted_iota(jnp.int32, (ROW_TILE, 1), 0)
    real = jnp.logical_or(tb > 0, row >= PADF)
    x = jnp.where(real, x_ref[0], 0.0)
    xe_sc[8:8 + ROW_TILE, :] = x
    cw = cw_ref[...]
    xc = (cw[3:4] * x + cw[2:3] * xe_sc[7:7 + ROW_TILE, :] + cw[1:2] * xe_sc[6:6 + ROW_TILE, :]
          + cw[0:1] * xe_sc[5:5 + ROW_TILE, :] + cb_ref[...])
    xe_sc[0:8, :] = xe_sc[ROW_TILE:ROW_TILE + 8, :]

    xcb = xc.astype(jnp.bfloat16)
    lam = lam_ref[...]
    z = -lam
    softplus = jnp.maximum(z, 0.0) + jnp.log1p(jnp.exp(-jnp.abs(z)))
    for n in range(LRU_BLOCKS):
        sl = slice(n * LRU_BLOCK, (n + 1) * LRU_BLOCK)
        xn = xcb[:, sl]
        r = _sigmoid(jnp.dot(xn, wa_ref[n], preferred_element_type=jnp.float32) + ba_ref[:, sl])
        i = _sigmoid(jnp.dot(xn, wx_ref[n], preferred_element_type=jnp.float32) + bx_ref[:, sl])
        log_a = -LRU_C * r * softplus[:, sl]
        a = jnp.exp(log_a)
        a_sc[:, sl] = a
        u = jnp.sqrt(-jnp.tanh(log_a) * (a * a + 1.0)) * (i * xc[:, sl])
        u_sc[:, sl] = jnp.where(real, u, 0.0)

    groups = ROW_TILE // 8
    a = a_sc[...].reshape(groups, 8, D_RNN)
    u = u_sc[...].reshape(groups, 8, D_RNN)
    sub = lax.broadcasted_iota(jnp.int32, (1, 8, 1), 1)
    for d in (1, 2, 4):
        later = sub >= d
        u = jnp.where(later, a * pltpu.roll(u, d, axis=1) + u, u)
        a = jnp.where(later, a * pltpu.roll(a, d, axis=1), a)
    h = h_sc[...]
    for gi in range(groups):
        hg = a[gi] * h + u[gi]
        u_sc[gi * 8:(gi + 1) * 8, :] = hg
        h = hg[7:8, :]
    h_sc[...] = h

    o_ref[0] = (u_sc[...] * g_ref[0].astype(jnp.float32)).astype(jnp.bfloat16)


def _recurrent(x_lru, g_lru, conv_w, conv_b, w_a, b_a, w_x, b_x, lam):
    B, tp, _ = x_lru.shape
    blk = pl.BlockSpec((1, ROW_TILE, D_RNN), lambda bi, t: (bi, t, 0))
    vec = pl.BlockSpec((1, D_RNN), lambda bi, t: (0, 0))
    wblk = pl.BlockSpec((LRU_BLOCKS, LRU_BLOCK, LRU_BLOCK), lambda bi, t: (0, 0, 0))
    return pl.pallas_call(
        _rec_kernel,
        grid=(B, tp // ROW_TILE),
        in_specs=[blk, blk, pl.BlockSpec((CONV_WIDTH, D_RNN), lambda bi, t: (0, 0)), vec,
                  wblk, vec, wblk, vec, vec],
        out_specs=blk,
        out_shape=jax.ShapeDtypeStruct((B, tp, D_RNN), jnp.bfloat16),
        scratch_shapes=[
            pltpu.VMEM((ROW_TILE + 8, D_RNN), jnp.float32),
            pltpu.VMEM((ROW_TILE, D_RNN), jnp.float32),
            pltpu.VMEM((ROW_TILE, D_RNN), jnp.float32),
            pltpu.VMEM((1, D_RNN), jnp.float32),
        ],
        compiler_params=_cparams(("parallel", "arbitrary")),
        name="recurrent",
    )(x_lru, g_lru, conv_w, conv_b, w_a, b_a, w_x, b_x, lam)


N_COLS = N_HEADS * ROW_TILE
COL_TILE = 2 * ROW_TILE
LOWEST = -3.0e38


def _ordered_bits_to_f32(k):
    return pltpu.bitcast(k ^ (lax.shift_right_arithmetic(k, 31) & 0x7FFFFFFF), jnp.float32)


def _attn_kernel(k_top, n_blocks, qi_ref, wi_ref, q_ref, kidx_ref, ckv_ref, ckvt_ref, wuk_ref, wuv_ref, tril_ref,
                 o_ref, sc_sc, sc16_sc, qit_sc, qabst_sc, acct_sc, m_sc, l_sc, alpha_sc, p_sc, s0_sc, thr_sc,
                 excess_sc):
    t = pl.program_id(1)
    n_chunks = sc_sc.shape[1]
    chunk_blocks = KEY_CHUNK // ROW_TILE
    blk_s = jnp.minimum(t, n_blocks - 1)
    nch_s = blk_s // chunk_blocks + 1
    nch_a = jnp.where(t > 0, (t - 1) // chunk_blocks + 1, 0)
    slot_s = lax.rem(t, 2)
    slot_a = 1 - slot_s
    q_pos = blk_s * ROW_TILE + lax.broadcasted_iota(jnp.int32, (1, ROW_TILE), 1)
    k_iota = lax.broadcasted_iota(jnp.int32, (KEY_CHUNK, 1), 0)

    qit = qi_ref[0].astype(jnp.float32).T
    for h in range(IDX_HEADS):
        qit_sc[:, h * ROW_TILE:(h + 1) * ROW_TILE] = qit[h * IDX_DIM:(h + 1) * IDX_DIM, :].astype(jnp.bfloat16)
    wit = wi_ref[0].T
    wi_row = jnp.concatenate([wit[h:h + 1, :] for h in range(IDX_HEADS)], axis=1)
    qt = q_ref[0].astype(jnp.float32).T
    scale = HEAD_DIM ** -0.5 * math.log2(math.e)
    for h in range(N_HEADS):
        qa = jnp.dot(wuk_ref[h], qt[h * HEAD_DIM:(h + 1) * HEAD_DIM, :].astype(jnp.bfloat16),
                     preferred_element_type=jnp.float32)
        qabst_sc[:, h * ROW_TILE:(h + 1) * ROW_TILE] = (qa * scale).astype(jnp.bfloat16)

    def score_chunk(c):
        ks = kidx_ref[0, c][:, 0:IDX_DIM]
        sc = None
        for j in range(N_COLS // COL_TILE):
            cols = slice(j * COL_TILE, (j + 1) * COL_TILE)
            lg = jnp.dot(ks, qit_sc[:, cols], preferred_element_type=jnp.float32)
            w = jnp.maximum(lg, 0.0) * wi_row[:, cols]
            part = w[:, :ROW_TILE] + w[:, ROW_TILE:]
            sc = part if sc is None else sc + part
        k_pos = c * KEY_CHUNK + k_iota
        visible = jnp.logical_and(k_pos <= q_pos, k_pos >= PADF)
        sc = jnp.where(visible, sc, -jnp.inf)
        sc_sc[slot_s, c] = sc
        sc16_sc[slot_s, c] = sc.astype(jnp.bfloat16)

    @pl.when(t == 0)
    def _():
        thr_sc[...] = jnp.full_like(thr_sc, LOWEST)

    thr_a = thr_sc[...]
    n_tiles = N_COLS // COL_TILE
    last = slice((n_tiles - 1) * COL_TILE, n_tiles * COL_TILE)
    m_sc[...] = jnp.full_like(m_sc, NEG_INF)
    l_sc[...] = jnp.zeros_like(l_sc)
    acct_sc[...] = jnp.zeros_like(acct_sc)
    alpha_sc[:, last] = jnp.ones((1, COL_TILE), jnp.float32)
    p_sc[:, last] = jnp.zeros((KEY_CHUNK, COL_TILE), jnp.bfloat16)

    def logits_tile(kc, j):
        cols = slice(j * COL_TILE, (j + 1) * COL_TILE)
        return jnp.dot(kc, qabst_sc[:, cols], preferred_element_type=jnp.float32)

    def softmax_tile(s, sel, j):
        cols = slice(j * COL_TILE, (j + 1) * COL_TILE)
        s = jnp.concatenate([jnp.where(sel, s[:, :ROW_TILE], NEG_INF),
                             jnp.where(sel, s[:, ROW_TILE:], NEG_INF)], axis=1)
        m_prev = m_sc[:, cols]
        m_new = jnp.maximum(m_prev, jnp.max(s, axis=0, keepdims=True))
        p = jnp.exp2(s - m_new)
        alpha = jnp.exp2(m_prev - m_new)
        l_sc[:, cols] = alpha * l_sc[:, cols] + jnp.sum(p, axis=0, keepdims=True)
        m_sc[:, cols] = m_new
        alpha_sc[:, cols] = alpha
        p_sc[:, cols] = p.astype(jnp.bfloat16)

    def value_tile(kct, j):
        cols = slice(j * COL_TILE, (j + 1) * COL_TILE)
        acct_sc[:, cols] = alpha_sc[:, cols] * acct_sc[:, cols] + jnp.dot(
            kct, p_sc[:, cols], preferred_element_type=jnp.float32)

    def attn_chunk(c):
        kc = ckv_ref[0, c]
        kct = ckvt_ref[0, c]
        sel = sc_sc[slot_a, c] >= thr_a
        softmax_tile(s0_sc[...], sel, 0)
        value_tile(ckvt_ref[0, jnp.maximum(c - 1, 0)], n_tiles - 1)
        for j in range(1, n_tiles):
            softmax_tile(logits_tile(kc, j), sel, j)
            value_tile(kct, j - 1)
        s0_sc[...] = logits_tile(ckv_ref[0, jnp.minimum(c + 1, n_chunks - 1)], 0)

    def both(c, carry):
        attn_chunk(c)
        score_chunk(c)
        return carry

    def score_only(c, carry):
        score_chunk(c)
        return carry

    s0_sc[...] = logits_tile(ckv_ref[0, 0], 0)
    lax.fori_loop(0, nch_a, both, 0)
    lax.fori_loop(nch_a, nch_s, score_only, 0)

    @pl.when(t > 0)
    def _():
        value_tile(ckvt_ref[0, nch_a - 1], n_tiles - 1)
        o_lat_t = acct_sc[...] / l_sc[...]
        for h in range(N_HEADS):
            o_lat = o_lat_t[:, h * ROW_TILE:(h + 1) * ROW_TILE].T.astype(jnp.bfloat16)
            oh = jnp.dot(o_lat, wuv_ref[h], preferred_element_type=jnp.float32)
            o_ref[0, :, h * HEAD_DIM:(h + 1) * HEAD_DIM] = oh.astype(jnp.bfloat16)

    def threshold_search(n):
        def count_ge(cand):
            acc = jnp.zeros((8, ROW_TILE), jnp.int32)
            for c in range(n):
                hit = jnp.where(sc_sc[slot_s, c] >= cand, 1, 0)
                acc = acc + jnp.sum(hit.reshape(KEY_CHUNK // 8, 8, ROW_TILE), axis=0)
            return jnp.sum(acc, axis=0, keepdims=True)

        def count16_ge(cand):
            one = jnp.ones((), jnp.bfloat16)
            zero = jnp.zeros((), jnp.bfloat16)
            acc = jnp.zeros((16, ROW_TILE), jnp.float32)
            for c in range(n):
                hit = jnp.where(sc16_sc[slot_s, c] >= cand, one, zero).reshape(KEY_CHUNK // 16, 16, ROW_TILE)
                parts = [hit[i] for i in range(KEY_CHUNK // 16)]
                while len(parts) > 1:
                    parts = [parts[i] + parts[i + 1] for i in range(0, len(parts), 2)]
                acc = acc + parts[0].astype(jnp.float32)
            return jnp.sum(acc, axis=0, keepdims=True)

        def coarse_pass(i, base):
            cand = base + lax.shift_left(jnp.int32(1), 15 - i)
            raw = cand ^ (lax.shift_right_arithmetic(cand, 15) & 0x7FFF)
            cand_f = pltpu.bitcast(lax.shift_left(raw, 16), jnp.float32)
            cnt = count16_ge(cand_f.astype(jnp.bfloat16))
            return jnp.where(cnt >= k_top, cand, base)

        v16 = lax.fori_loop(0, 16, coarse_pass, jnp.full((1, ROW_TILE), -(2 ** 15), jnp.int32))
        few = v16 == -(2 ** 15)

        centre = lax.shift_left(v16, 16) + jnp.where(v16 < 0, 2 ** 16 - 1, 0)
        lo_k = centre - (2 ** 15 + 1)
        span = 3 * 2 ** 15 + 1

        def fine_pass(i, state):
            off, n_ge = state
            cand_off = off + lax.shift_left(jnp.int32(1), 16 - i)
            cnt = count_ge(_ordered_bits_to_f32(lo_k + cand_off))
            keep = jnp.logical_and(cnt >= k_top, cand_off < span)
            return jnp.where(keep, cand_off, off), jnp.where(keep, cnt, n_ge)

        off, n_ge = lax.fori_loop(0, 17, fine_pass, (jnp.zeros((1, ROW_TILE), jnp.int32),
                                                     jnp.zeros((1, ROW_TILE), jnp.int32)))
        thr_sc[...] = jnp.where(few, LOWEST, _ordered_bits_to_f32(lo_k + off))
        excess_sc[...] = jnp.where(jnp.logical_and(n_ge > k_top, jnp.logical_not(few)), 1, 0)

    for n in range(1, n_chunks + 1):
        pl.when(jnp.logical_and(nch_s == n, t < n_blocks))(functools.partial(threshold_search, n))
    thr = thr_sc[...]
    excess = excess_sc[...] > 0

    @pl.when(jnp.logical_and(jnp.max(jnp.where(excess, 1, 0)) > 0, t < n_blocks))
    def _():
        def above(c, acc):
            hit = jnp.where(sc_sc[slot_s, c] > thr, 1, 0)
            return acc + jnp.sum(hit.reshape(KEY_CHUNK // 8, 8, ROW_TILE), axis=0)
        acc = lax.fori_loop(0, nch_s, above, jnp.zeros((8, ROW_TILE), jnp.int32))
        need = (k_top - jnp.sum(acc, axis=0, keepdims=True)).astype(jnp.float32)

        def demote(c, seen):
            s = sc_sc[slot_s, c]
            tied = s == thr
            rank = seen + jnp.dot(tril_ref[...], jnp.where(tied, 1.0, 0.0).astype(jnp.bfloat16),
                                  preferred_element_type=jnp.float32)
            drop = jnp.logical_and(jnp.logical_and(excess, tied), rank > need)
            sc_sc[slot_s, c] = jnp.where(drop, -jnp.inf, s)
            return rank[KEY_CHUNK - 1:KEY_CHUNK, :]

        lax.fori_loop(0, nch_s, demote, jnp.zeros((1, ROW_TILE), jnp.float32))


def _sparse_attention(qidx, widx, q, kidx_c, ckv_c, ckvt_c, wuk, wuv, k_top):
    B, tp, _ = q.shape
    nch = kidx_c.shape[1]
    nblk = tp // ROW_TILE

    def scored(bi, t):
        return (bi, jnp.minimum(t, nblk - 1), 0)

    def attended(bi, t):
        return (bi, jnp.maximum(t - 1, 0), 0)

    return pl.pallas_call(
        functools.partial(_attn_kernel, k_top, nblk),
        grid=(B, nblk + 1),
        in_specs=[
            pl.BlockSpec((1, ROW_TILE, P_QIDX), scored),
            pl.BlockSpec((1, ROW_TILE, P_WIDX), scored),
            pl.BlockSpec((1, ROW_TILE, P_Q), attended),
            pl.BlockSpec((1, nch, KEY_CHUNK, P_KIDX), lambda bi, t: (bi, 0, 0, 0)),
            pl.BlockSpec((1, nch, KEY_CHUNK, P_CKV), lambda bi, t: (bi, 0, 0, 0)),
            pl.BlockSpec((1, nch, P_CKV, KEY_CHUNK), lambda bi, t: (bi, 0, 0, 0)),
            _resident((N_HEADS, KV_RANK, HEAD_DIM)),
            _resident((N_HEADS, KV_RANK, HEAD_DIM)),
            _resident((KEY_CHUNK, KEY_CHUNK)),
        ],
        out_specs=pl.BlockSpec((1, ROW_TILE, P_Q), attended),
        out_shape=jax.ShapeDtypeStruct((B, tp, P_Q), jnp.bfloat16),
        scratch_shapes=[
            pltpu.VMEM((2, nch, KEY_CHUNK, ROW_TILE), jnp.float32),
            pltpu.VMEM((2, nch, KEY_CHUNK, ROW_TILE), jnp.bfloat16),
            pltpu.VMEM((IDX_DIM, N_COLS), jnp.bfloat16),
            pltpu.VMEM((KV_RANK, N_COLS), jnp.bfloat16),
            pltpu.VMEM((KV_RANK, N_COLS), jnp.float32),
            pltpu.VMEM((1, N_COLS), jnp.float32),
            pltpu.VMEM((1, N_COLS), jnp.float32),
            pltpu.VMEM((1, N_COLS), jnp.float32),
            pltpu.VMEM((KEY_CHUNK, N_COLS), jnp.bfloat16),
            pltpu.VMEM((KEY_CHUNK, COL_TILE), jnp.float32),
            pltpu.VMEM((1, ROW_TILE), jnp.float32),
            pltpu.VMEM((1, ROW_TILE), jnp.int32),
        ],
        compiler_params=_cparams(("parallel", "arbitrary")),
        name="sparse_attn",
    )(qidx, widx, q, kidx_c, ckv_c, ckvt_c, wuk, wuv, jnp.tril(jnp.ones((KEY_CHUNK, KEY_CHUNK), jnp.bfloat16)))


def _merge_kernel(alpha, x_ref, hg_ref, o_ref, ga_ref, gb_ref, wa_ref, wb_ref, wo_ref, g_ref, b_ref, y_ref):
    ya = jnp.dot(hg_ref[...], wa_ref[...], preferred_element_type=jnp.float32)
    yb = jnp.dot(o_ref[...], wb_ref[...], preferred_element_type=jnp.float32)
    mixed = ga_ref[...] * ya + gb_ref[...] * yb
    z = jnp.dot(mixed.astype(jnp.bfloat16), wo_ref[...], preferred_element_type=jnp.float32)
    y_ref[...] = _layer_norm(alpha * x_ref[...] + z, g_ref[...], b_ref[...])


def _merge(alpha, x2d, hg, o, ga, gb, w_a, w_b, w_o, g, b, tm):
    n = x2d.shape[0]

    def rows(w):
        return pl.BlockSpec((tm, w), lambda i: (i, 0))

    return pl.pallas_call(
        functools.partial(_merge_kernel, alpha),
        grid=(n // tm,),
        in_specs=[rows(D_MODEL), rows(D_RNN), rows(P_Q), rows(D_MODEL), rows(D_MODEL),
                  _resident((D_RNN, D_MODEL)), _resident((P_Q, D_MODEL)), _resident((D_MODEL, D_MODEL)),
                  _resident((1, D_MODEL)), _resident((1, D_MODEL))],
        out_specs=rows(D_MODEL),
        out_shape=jax.ShapeDtypeStruct((n, D_MODEL), jnp.float32),
        compiler_params=_cparams(("parallel",)),
        name="merge",
    )(x2d, hg, o, ga, gb, w_a, w_b, w_o, g, b)


def _mlp_kernel(alpha, x_ref, wu_ref, bu_ref, wd_ref, bd_ref, g_ref, b_ref, y_ref):
    x = x_ref[...]
    h = jnp.dot(x.astype(jnp.bfloat16), wu_ref[...], preferred_element_type=jnp.float32) + bu_ref[...]
    h = jnp.maximum(h, 0.0)
    h = (h * h).astype(jnp.bfloat16)
    z = jnp.dot(h, wd_ref[...], preferred_element_type=jnp.float32) + bd_ref[...]
    y_ref[...] = _layer_norm(alpha * x + z, g_ref[...], b_ref[...])


def _mlp(alpha, x2d, w_up, b_up, w_down, b_down, g, b, tm):
    n = x2d.shape[0]
    return pl.pallas_call(
        functools.partial(_mlp_kernel, alpha),
        grid=(n // tm,),
        in_specs=[pl.BlockSpec((tm, D_MODEL), lambda i: (i, 0)),
                  _resident((D_MODEL, D_FF)), _resident((1, D_FF)),
                  _resident((D_FF, D_MODEL)), _resident((1, D_MODEL)),
                  _resident((1, D_MODEL)), _resident((1, D_MODEL))],
        out_specs=pl.BlockSpec((tm, D_MODEL), lambda i: (i, 0)),
        out_shape=jax.ShapeDtypeStruct((n, D_MODEL), jnp.float32),
        compiler_params=_cparams(("parallel",)),
        name="mlp",
    )(x2d, w_up, b_up, w_down, b_down, g, b)


def _pad_in_proj(w_in):
    splits = (D_RNN, D_RNN, N_HEADS * HEAD_DIM, KV_RANK, IDX_HEADS * IDX_DIM, IDX_DIM, IDX_HEADS,
              D_MODEL, D_MODEL)
    padded = (P_LRU, P_G, P_Q, P_CKV, P_QIDX, P_KIDX, P_WIDX, P_GA, P_GB)
    parts, off = [], 0
    for w, pw in zip(splits, padded):
        part = w_in[:, off:off + w]
        if pw > w:
            part = jnp.pad(part, ((0, 0), (0, pw - w)))
        parts.append(part)
        off += w
    return jnp.concatenate(parts, axis=1).astype(jnp.bfloat16)


def kernel(x, meta_tokens, ln_in_g, ln_in_b, w_in, conv_w, conv_b, w_rg_a, b_rg_a, w_rg_x, b_rg_x, lru_lambda,
           kv_norm_g, w_uk, w_uv, w_branch_a, w_branch_b, w_out, ln1_g, ln1_b, w_up, b_up, w_down, b_down,
           ln2_g, ln2_b):
    B, S, _ = x.shape
    depth = w_in.shape[0]
    assert S % ROW_TILE == 0
    T = S + N_META
    k_top = min(TOPK_MAX, T // 4)
    alpha = (2.0 * depth) ** 0.25
    n_blk = S // ROW_TILE + 1
    tp = n_blk * ROW_TILE
    tk = -(-tp // KEY_CHUNK) * KEY_CHUNK
    n = B * tp
    tm_proj, tm = 320, 640
    assert n % tm_proj == 0 and n % tm == 0
    bf16 = jnp.bfloat16

    def vec(a):
        return a.reshape(1, -1)

    meta_pad = jnp.pad(meta_tokens.astype(x.dtype), ((PADF, 0), (0, 0)))
    h = _ln_in(x, meta_pad, vec(ln_in_g), vec(ln_in_b), n_blk).reshape(n, D_MODEL)

    for l in range(depth):
        lru, g_lru, q, ckv, qidx, kidx, widx, ga, gb = _proj(h, _pad_in_proj(w_in[l]), vec(kv_norm_g[l]), tm_proj)

        hg = _recurrent(lru.reshape(B, tp, D_RNN), g_lru.reshape(B, tp, D_RNN), conv_w[l], vec(conv_b[l]),
                        w_rg_a[l].astype(bf16), vec(b_rg_a[l]), w_rg_x[l].astype(bf16), vec(b_rg_x[l]),
                        vec(lru_lambda[l]))

        key_pad = ((0, 0), (0, tk - tp), (0, 0))
        kidx_c = jnp.pad(kidx.reshape(B, tp, P_KIDX), key_pad).reshape(B, tk // KEY_CHUNK, KEY_CHUNK, P_KIDX)
        ckv_c = jnp.pad(ckv.reshape(B, tp, P_CKV), key_pad).reshape(B, tk // KEY_CHUNK, KEY_CHUNK, P_CKV)
        o = _sparse_attention(qidx.reshape(B, tp, P_QIDX), widx.reshape(B, tp, P_WIDX), q.reshape(B, tp, P_Q),
                              kidx_c, ckv_c, jnp.swapaxes(ckv_c, 2, 3),
                              w_uk[l].astype(bf16), w_uv[l].astype(bf16), k_top)

        h = _merge(alpha, h, hg.reshape(n, D_RNN), o.reshape(n, P_Q), ga, gb, w_branch_a[l].astype(bf16),
                   w_branch_b[l].astype(bf16), w_out[l].astype(bf16), vec(ln1_g[l]), vec(ln1_b[l]), tm)
        h = _mlp(alpha, h, w_up[l].astype(bf16), vec(b_up[l]), w_down[l].astype(bf16), vec(b_down[l]),
                 vec(ln2_g[l]), vec(ln2_b[l]), tm)

    return h.reshape(B, tp, D_MODEL)[:, ROW_TILE:, :]
```

```python
import functools
import math

import jax
import jax.numpy as jnp
from jax import lax
from jax.experimental import pallas as pl
from jax.experimental.pallas import tpu as pltpu

D_MODEL = 1024
N_META = 16
D_RNN = 1280
LRU_BLOCKS = 10
LRU_BLOCK = D_RNN // LRU_BLOCKS
CONV_WIDTH = 4
LRU_C = 8.0
N_HEADS = 8
HEAD_DIM = 128
KV_RANK = 256
IDX_HEADS = 8
IDX_DIM = 64
TOPK_MAX = 256
D_FF = 4 * D_MODEL
LN_EPS = 1e-5
NEG_INF = -1e30

LANES = 128
ROW_TILE = 128
PADF = ROW_TILE - N_META
KEY_CHUNK = 512
VMEM_LIMIT = 56 * 1024 * 1024

P_LRU, P_G, P_Q, P_CKV, P_QIDX, P_KIDX, P_WIDX, P_GA, P_GB = (
    D_RNN, D_RNN, N_HEADS * HEAD_DIM, KV_RANK, IDX_HEADS * IDX_DIM, LANES, LANES, D_MODEL, D_MODEL)
P_OFFS = []
_acc = 0
for _w in (P_LRU, P_G, P_Q, P_CKV, P_QIDX, P_KIDX, P_WIDX, P_GA, P_GB):
    P_OFFS.append(_acc)
    _acc += _w
P_TOTAL = _acc


def _cparams(sem):
    return pltpu.CompilerParams(dimension_semantics=sem, vmem_limit_bytes=VMEM_LIMIT)


def _resident(shape):
    nd = len(shape)
    return pl.BlockSpec(shape, lambda *_: (0,) * nd, pipeline_mode=pl.Buffered(1))


def _layer_norm(x, g, b):
    mu = jnp.mean(x, axis=-1, keepdims=True)
    d = x - mu
    var = jnp.mean(d * d, axis=-1, keepdims=True)
    return d * lax.rsqrt(var + LN_EPS) * g + b


def _sigmoid(x):
    return 0.5 * jnp.tanh(0.5 * x) + 0.5


def _proj_kernel(normalize, x_ref, w_ref, kvg_ref, lng_ref, lnb_ref, *out_refs):
    if normalize:
        h_ref, *out_refs = out_refs
        h = _layer_norm(x_ref[...], lng_ref[...], lnb_ref[...])
        h_ref[...] = h
        xb = h.astype(jnp.bfloat16)
    else:
        xb = x_ref[...].astype(jnp.bfloat16)
    lru_ref, g_ref, q_ref, ckv_ref, qidx_ref, kidx_ref, widx_ref, ga_ref, gb_ref = out_refs

    def seg(i, width):
        return jnp.dot(xb, w_ref[:, P_OFFS[i]:P_OFFS[i] + width], preferred_element_type=jnp.float32)

    lru_ref[...] = seg(0, P_LRU)
    g = seg(1, P_G)
    gelu = 0.5 * g * (1.0 + jnp.tanh(math.sqrt(2.0 / math.pi) * (g + 0.044715 * (g * g * g))))
    g_ref[...] = gelu.astype(jnp.bfloat16)
    q_ref[...] = seg(2, P_Q).astype(jnp.bfloat16)

    c = seg(3, P_CKV)
    ms = jnp.mean(c * c, axis=-1, keepdims=True)
    ckv_ref[...] = (c * lax.rsqrt(ms + LN_EPS) * kvg_ref[...]).astype(jnp.bfloat16)

    qidx_ref[...] = seg(4, P_QIDX).astype(jnp.bfloat16)

    k = seg(5, P_KIDX)
    lane = lax.broadcasted_iota(jnp.int32, k.shape, 1)
    mu = jnp.sum(k, axis=-1, keepdims=True) * (1.0 / IDX_DIM)
    d = jnp.where(lane < IDX_DIM, k - mu, 0.0)
    var = jnp.sum(d * d, axis=-1, keepdims=True) * (1.0 / IDX_DIM)
    kidx_ref[...] = (d * lax.rsqrt(var + LN_EPS)).astype(jnp.bfloat16)

    widx_ref[...] = seg(6, P_WIDX) * (IDX_HEADS ** -0.5 * IDX_DIM ** -0.5)
    ga_ref[...] = _sigmoid(seg(7, P_GA))
    gb_ref[...] = _sigmoid(seg(8, P_GB))


def _proj(x2d, w_in_p, kv_g, ln_g, ln_b, tm, normalize):
    n = x2d.shape[0]
    f32, bf16 = jnp.float32, jnp.bfloat16
    widths = (P_LRU, P_G, P_Q, P_CKV, P_QIDX, P_KIDX, P_WIDX, P_GA, P_GB)
    dtypes = (f32, bf16, bf16, bf16, bf16, bf16, f32, f32, f32)
    if normalize:
        widths, dtypes = (D_MODEL,) + widths, (f32,) + dtypes
    return pl.pallas_call(
        functools.partial(_proj_kernel, normalize),
        grid=(n // tm,),
        in_specs=[
            pl.BlockSpec((tm, D_MODEL), lambda i: (i, 0)),
            _resident((D_MODEL, P_TOTAL)),
            _resident((1, KV_RANK)),
            _resident((1, D_MODEL)),
            _resident((1, D_MODEL)),
        ],
        out_specs=[pl.BlockSpec((tm, w), lambda i: (i, 0)) for w in widths],
        out_shape=[jax.ShapeDtypeStruct((n, w), dt) for w, dt in zip(widths, dtypes)],
        compiler_params=_cparams(("parallel",)),
        name="proj",
    )(x2d, w_in_p, kv_g, ln_g, ln_b)


def _rec_kernel(x_ref, g_ref, cw_ref, cb_ref, wa_ref, ba_ref, wx_ref, bx_ref, lam_ref, o_ref,
                xe_sc, a_sc, u_sc, h_sc):
    tb = pl.program_id(1)

    @pl.when(tb == 0)
    def _():
        xe_sc[0:8, :] = jnp.zeros((8, D_RNN), jnp.float32)
        h_sc[...] = jnp.zeros_like(h_sc)

    row = lax.broadcasted_iota(jnp.int32, (ROW_TILE, 1), 0)
    real = jnp.logical_or(tb > 0, row >= PADF)
    x = jnp.where(real, x_ref[0], 0.0)
    xe_sc[8:8 + ROW_TILE, :] = x
    cw = cw_ref[...]
    xc = (cw[3:4] * x + cw[2:3] * xe_sc[7:7 + ROW_TILE, :] + cw[1:2] * xe_sc[6:6 + ROW_TILE, :]
          + cw[0:1] * xe_sc[5:5 + ROW_TILE, :] + cb_ref[...])
    xe_sc[0:8, :] = xe_sc[ROW_TILE:ROW_TILE + 8, :]

    xcb = xc.astype(jnp.bfloat16)
    lam = lam_ref[...]
    z = -lam
    softplus = jnp.maximum(z, 0.0) + jnp.log1p(jnp.exp(-jnp.abs(z)))
    for n in range(LRU_BLOCKS):
        sl = slice(n * LRU_BLOCK, (n + 1) * LRU_BLOCK)
        xn = xcb[:, sl]
        r = _sigmoid(jnp.dot(xn, wa_ref[n], preferred_element_type=jnp.float32) + ba_ref[:, sl])
        i = _sigmoid(jnp.dot(xn, wx_ref[n], preferred_element_type=jnp.float32) + bx_ref[:, sl])
        log_a = -LRU_C * r * softplus[:, sl]
        a = jnp.exp(log_a)
        a_sc[:, sl] = a
        u = jnp.sqrt(-jnp.tanh(log_a) * (a * a + 1.0)) * (i * xc[:, sl])
        u_sc[:, sl] = jnp.where(real, u, 0.0)

    groups = ROW_TILE // 8
    a = a_sc[...].reshape(groups, 8, D_RNN)
    u = u_sc[...].reshape(groups, 8, D_RNN)
    sub = lax.broadcasted_iota(jnp.int32, (1, 8, 1), 1)
    for d in (1, 2, 4):
        later = sub >= d
        u = jnp.where(later, a * pltpu.roll(u, d, axis=1) + u, u)
        a = jnp.where(later, a * pltpu.roll(a, d, axis=1), a)
    h = h_sc[...]
    for gi in range(groups):
        hg = a[gi] * h + u[gi]
        u_sc[gi * 8:(gi + 1) * 8, :] = hg
        h = hg[7:8, :]
    h_sc[...] = h

    o_ref[0] = (u_sc[...] * g_ref[0].astype(jnp.float32)).astype(jnp.bfloat16)


def _recurrent(x_lru, g_lru, conv_w, conv_b, w_a, b_a, w_x, b_x, lam):
    B, tp, _ = x_lru.shape
    blk = pl.BlockSpec((1, ROW_TILE, D_RNN), lambda bi, t: (bi, t, 0))
    vec = pl.BlockSpec((1, D_RNN), lambda bi, t: (0, 0))
    wblk = pl.BlockSpec((LRU_BLOCKS, LRU_BLOCK, LRU_BLOCK), lambda bi, t: (0, 0, 0))
    return pl.pallas_call(
        _rec_kernel,
        grid=(B, tp // ROW_TILE),
        in_specs=[blk, blk, pl.BlockSpec((CONV_WIDTH, D_RNN), lambda bi, t: (0, 0)), vec,
                  wblk, vec, wblk, vec, vec],
        out_specs=blk,
        out_shape=jax.ShapeDtypeStruct((B, tp, D_RNN), jnp.bfloat16),
        scratch_shapes=[
            pltpu.VMEM((ROW_TILE + 8, D_RNN), jnp.float32),
            pltpu.VMEM((ROW_TILE, D_RNN), jnp.float32),
            pltpu.VMEM((ROW_TILE, D_RNN), jnp.float32),
            pltpu.VMEM((1, D_RNN), jnp.float32),
        ],
        compiler_params=_cparams(("parallel", "arbitrary")),
        name="recurrent",
    )(x_lru, g_lru, conv_w, conv_b, w_a, b_a, w_x, b_x, lam)


N_COLS = N_HEADS * ROW_TILE
COL_TILE = 2 * ROW_TILE
LOWEST = -3.0e38


def _ordered_bits_to_f32(k):
    return pltpu.bitcast(k ^ (lax.shift_right_arithmetic(k, 31) & 0x7FFFFFFF), jnp.float32)


def _attn_kernel(k_top, n_blocks, qi_ref, wi_ref, q_ref, kidx_ref, ckv_ref, ckvt_ref, wuk_ref, wuv_ref, triu_ref,
                 o_ref, sc_sc, sc16_sc, qit_sc, qabst_sc, acct_sc, m_sc, l_sc, alpha_sc, p_sc, s0_sc, thr_sc,
                 excess_sc):
    t = pl.program_id(1)
    n_chunks = sc_sc.shape[1]
    chunk_blocks = KEY_CHUNK // ROW_TILE
    blk_s = jnp.minimum(t, n_blocks - 1)
    nch_s = blk_s // chunk_blocks + 1
    nch_a = jnp.where(t > 0, (t - 1) // chunk_blocks + 1, 0)
    slot_s = lax.rem(t, 2)
    slot_a = 1 - slot_s
    q_pos = blk_s * ROW_TILE + lax.broadcasted_iota(jnp.int32, (1, ROW_TILE), 1)
    k_iota = lax.broadcasted_iota(jnp.int32, (KEY_CHUNK, 1), 0)

    qit = qi_ref[0].astype(jnp.float32).T
    for h in range(IDX_HEADS):
        qit_sc[:, h * ROW_TILE:(h + 1) * ROW_TILE] = qit[h * IDX_DIM:(h + 1) * IDX_DIM, :].astype(jnp.bfloat16)
    wit = wi_ref[0].T
    wi_row = jnp.concatenate([wit[h:h + 1, :] for h in range(IDX_HEADS)], axis=1)
    qt = q_ref[0].astype(jnp.float32).T
    scale = HEAD_DIM ** -0.5 * math.log2(math.e)
    for h in range(N_HEADS):
        qa = jnp.dot(wuk_ref[h], qt[h * HEAD_DIM:(h + 1) * HEAD_DIM, :].astype(jnp.bfloat16),
                     preferred_element_type=jnp.float32)
        qabst_sc[:, h * ROW_TILE:(h + 1) * ROW_TILE] = (qa * scale).astype(jnp.bfloat16)

    def score_chunk(c):
        ks = kidx_ref[0, c][:, 0:IDX_DIM]
        sc = None
        for j in range(N_COLS // COL_TILE):
            cols = slice(j * COL_TILE, (j + 1) * COL_TILE)
            lg = jnp.dot(ks, qit_sc[:, cols], preferred_element_type=jnp.float32)
            w = jnp.maximum(lg, 0.0) * wi_row[:, cols]
            part = w[:, :ROW_TILE] + w[:, ROW_TILE:]
            sc = part if sc is None else sc + part
        k_pos = c * KEY_CHUNK + k_iota
        visible = jnp.logical_and(k_pos <= q_pos, k_pos >= PADF)
        sc = jnp.where(visible, sc, -jnp.inf)
        sc_sc[slot_s, c] = sc
        sc16_sc[slot_s, c] = sc.astype(jnp.bfloat16)

    @pl.when(t == 0)
    def _():
        thr_sc[...] = jnp.full_like(thr_sc, LOWEST)

    thr_a = thr_sc[...]
    n_tiles = N_COLS // COL_TILE
    last = slice((n_tiles - 1) * COL_TILE, n_tiles * COL_TILE)
    m_sc[...] = jnp.full_like(m_sc, NEG_INF)
    l_sc[...] = jnp.zeros_like(l_sc)
    acct_sc[...] = jnp.zeros_like(acct_sc)
    alpha_sc[:, last] = jnp.ones((1, COL_TILE), jnp.float32)
    p_sc[:, last] = jnp.zeros((KEY_CHUNK, COL_TILE), jnp.bfloat16)

    def logits_tile(kc, j):
        cols = slice(j * COL_TILE, (j + 1) * COL_TILE)
        return jnp.dot(kc, qabst_sc[:, cols], preferred_element_type=jnp.float32)

    def softmax_tile(s, sel, j):
        cols = slice(j * COL_TILE, (j + 1) * COL_TILE)
        s = jnp.concatenate([jnp.where(sel, s[:, :ROW_TILE], NEG_INF),
                             jnp.where(sel, s[:, ROW_TILE:], NEG_INF)], axis=1)
        m_prev = m_sc[:, cols]
        m_new = jnp.maximum(m_prev, jnp.max(s, axis=0, keepdims=True))
        p = jnp.exp2(s - m_new)
        alpha = jnp.exp2(m_prev - m_new)
        l_sc[:, cols] = alpha * l_sc[:, cols] + jnp.sum(p, axis=0, keepdims=True)
        m_sc[:, cols] = m_new
        alpha_sc[:, cols] = alpha
        p_sc[:, cols] = p.astype(jnp.bfloat16)

    def value_tile(kct, j):
        cols = slice(j * COL_TILE, (j + 1) * COL_TILE)
        acct_sc[:, cols] = alpha_sc[:, cols] * acct_sc[:, cols] + jnp.dot(
            kct, p_sc[:, cols], preferred_element_type=jnp.float32)

    def attn_chunk(c):
        kc = ckv_ref[0, c]
        kct = ckvt_ref[0, c]
        sel = sc_sc[slot_a, c] >= thr_a
        softmax_tile(s0_sc[...], sel, 0)
        value_tile(ckvt_ref[0, jnp.maximum(c - 1, 0)], n_tiles - 1)
        for j in range(1, n_tiles):
            softmax_tile(logits_tile(kc, j), sel, j)
            value_tile(kct, j - 1)
        s0_sc[...] = logits_tile(ckv_ref[0, jnp.minimum(c + 1, n_chunks - 1)], 0)

    def both(c, carry):
        attn_chunk(c)
        score_chunk(c)
        return carry

    def score_only(c, carry):
        score_chunk(c)
        return carry

    s0_sc[...] = logits_tile(ckv_ref[0, 0], 0)
    lax.fori_loop(0, nch_a, both, 0)
    lax.fori_loop(nch_a, nch_s, score_only, 0)

    @pl.when(t > 0)
    def _():
        value_tile(ckvt_ref[0, nch_a - 1], n_tiles - 1)
        o_lat_t = acct_sc[...] / l_sc[...]
        for h in range(N_HEADS):
            o_lat = o_lat_t[:, h * ROW_TILE:(h + 1) * ROW_TILE].T.astype(jnp.bfloat16)
            oh = jnp.dot(o_lat, wuv_ref[h], preferred_element_type=jnp.float32)
            o_ref[0, :, h * HEAD_DIM:(h + 1) * HEAD_DIM] = oh.astype(jnp.bfloat16)

    def threshold_search(n):
        def count_ge(cand):
            acc = jnp.zeros((8, ROW_TILE), jnp.int32)
            for c in range(n):
                hit = jnp.where(sc_sc[slot_s, c] >= cand, 1, 0)
                acc = acc + jnp.sum(hit.reshape(KEY_CHUNK // 8, 8, ROW_TILE), axis=0)
            return jnp.sum(acc, axis=0, keepdims=True)

        def count16_ge(cand):
            one = jnp.ones((), jnp.bfloat16)
            zero = jnp.zeros((), jnp.bfloat16)
            acc = jnp.zeros((16, ROW_TILE), jnp.float32)
            for c in range(n):
                hit = jnp.where(sc16_sc[slot_s, c] >= cand, one, zero).reshape(KEY_CHUNK // 16, 16, ROW_TILE)
                parts = [hit[i] for i in range(KEY_CHUNK // 16)]
                while len(parts) > 1:
                    parts = [parts[i] + parts[i + 1] for i in range(0, len(parts), 2)]
                acc = acc + parts[0].astype(jnp.float32)
            return jnp.sum(acc, axis=0, keepdims=True)

        def coarse_pass(i, base):
            cand = base + lax.shift_left(jnp.int32(1), 15 - i)
            raw = cand ^ (lax.shift_right_arithmetic(cand, 15) & 0x7FFF)
            cand_f = pltpu.bitcast(lax.shift_left(raw, 16), jnp.float32)
            cnt = count16_ge(cand_f.astype(jnp.bfloat16))
            return jnp.where(cnt >= k_top, cand, base)

        v16 = lax.fori_loop(0, 16, coarse_pass, jnp.full((1, ROW_TILE), -(2 ** 15), jnp.int32))
        few = v16 == -(2 ** 15)

        centre = lax.shift_left(v16, 16) + jnp.where(v16 < 0, 2 ** 16 - 1, 0)
        lo_k = centre - (2 ** 15 + 1)
        span = 3 * 2 ** 15 + 1

        def fine_pass(i, state):
            off, n_ge = state
            cand_off = off + lax.shift_left(jnp.int32(1), 16 - i)
            cnt = count_ge(_ordered_bits_to_f32(lo_k + cand_off))
            keep = jnp.logical_and(cnt >= k_top, cand_off < span)
            return jnp.where(keep, cand_off, off), jnp.where(keep, cnt, n_ge)

        off, n_ge = lax.fori_loop(0, 17, fine_pass, (jnp.zeros((1, ROW_TILE), jnp.int32),
                                                     jnp.zeros((1, ROW_TILE), jnp.int32)))
        thr_sc[...] = jnp.where(few, LOWEST, _ordered_bits_to_f32(lo_k + off))
        excess_sc[...] = jnp.where(few, 0, jnp.maximum(n_ge - k_top, 0))

    for n in range(1, n_chunks + 1):
        pl.when(jnp.logical_and(nch_s == n, t < n_blocks))(functools.partial(threshold_search, n))
    thr = thr_sc[...]
    excess = excess_sc[...]

    @pl.when(jnp.logical_and(jnp.max(excess) > 0, t < n_blocks))
    def _():
        surplus = excess.astype(jnp.float32)

        def demote(i, after):
            c = nch_s - 1 - i
            s = sc_sc[slot_s, c]
            tied = s == thr
            rank = after + jnp.dot(triu_ref[...], jnp.where(tied, 1.0, 0.0).astype(jnp.bfloat16),
                                   preferred_element_type=jnp.float32)
            sc_sc[slot_s, c] = jnp.where(jnp.logical_and(tied, rank <= surplus), -jnp.inf, s)
            return rank[0:1, :]

        lax.fori_loop(0, nch_s, demote, jnp.zeros((1, ROW_TILE), jnp.float32))


def _sparse_attention(qidx, widx, q, kidx_c, ckv_c, ckvt_c, wuk, wuv, k_top):
    B, tp, _ = q.shape
    nch = kidx_c.shape[1]
    nblk = tp // ROW_TILE

    def scored(bi, t):
        return (bi, jnp.minimum(t, nblk - 1), 0)

    def attended(bi, t):
        return (bi, jnp.maximum(t - 1, 0), 0)

    return pl.pallas_call(
        functools.partial(_attn_kernel, k_top, nblk),
        grid=(B, nblk + 1),
        in_specs=[
            pl.BlockSpec((1, ROW_TILE, P_QIDX), scored),
            pl.BlockSpec((1, ROW_TILE, P_WIDX), scored),
            pl.BlockSpec((1, ROW_TILE, P_Q), attended),
            pl.BlockSpec((1, nch, KEY_CHUNK, P_KIDX), lambda bi, t: (bi, 0, 0, 0)),
            pl.BlockSpec((1, nch, KEY_CHUNK, P_CKV), lambda bi, t: (bi, 0, 0, 0)),
            pl.BlockSpec((1, nch, P_CKV, KEY_CHUNK), lambda bi, t: (bi, 0, 0, 0)),
            _resident((N_HEADS, KV_RANK, HEAD_DIM)),
            _resident((N_HEADS, KV_RANK, HEAD_DIM)),
            _resident((KEY_CHUNK, KEY_CHUNK)),
        ],
        out_specs=pl.BlockSpec((1, ROW_TILE, P_Q), attended),
        out_shape=jax.ShapeDtypeStruct((B, tp, P_Q), jnp.bfloat16),
        scratch_shapes=[
            pltpu.VMEM((2, nch, KEY_CHUNK, ROW_TILE), jnp.float32),
            pltpu.VMEM((2, nch, KEY_CHUNK, ROW_TILE), jnp.bfloat16),
            pltpu.VMEM((IDX_DIM, N_COLS), jnp.bfloat16),
            pltpu.VMEM((KV_RANK, N_COLS), jnp.bfloat16),
            pltpu.VMEM((KV_RANK, N_COLS), jnp.float32),
            pltpu.VMEM((1, N_COLS), jnp.float32),
            pltpu.VMEM((1, N_COLS), jnp.float32),
            pltpu.VMEM((1, N_COLS), jnp.float32),
            pltpu.VMEM((KEY_CHUNK, N_COLS), jnp.bfloat16),
            pltpu.VMEM((KEY_CHUNK, COL_TILE), jnp.float32),
            pltpu.VMEM((1, ROW_TILE), jnp.float32),
            pltpu.VMEM((1, ROW_TILE), jnp.int32),
        ],
        compiler_params=_cparams(("parallel", "arbitrary")),
        name="sparse_attn",
    )(qidx, widx, q, kidx_c, ckv_c, ckvt_c, wuk, wuv, jnp.triu(jnp.ones((KEY_CHUNK, KEY_CHUNK), jnp.bfloat16)))


def _merge_kernel(alpha, x_ref, hg_ref, o_ref, ga_ref, gb_ref, wa_ref, wb_ref, wo_ref, g_ref, b_ref, y_ref):
    half = x_ref.shape[0] // 2
    for r in (slice(0, half), slice(half, 2 * half)):
        ya = jnp.dot(hg_ref[r, :], wa_ref[...], preferred_element_type=jnp.float32)
        yb = jnp.dot(o_ref[r, :], wb_ref[...], preferred_element_type=jnp.float32)
        mixed = ga_ref[r, :] * ya + gb_ref[r, :] * yb
        z = jnp.dot(mixed.astype(jnp.bfloat16), wo_ref[...], preferred_element_type=jnp.float32)
        y_ref[r, :] = _layer_norm(alpha * x_ref[r, :] + z, g_ref[...], b_ref[...])


def _merge(alpha, x2d, hg, o, ga, gb, w_a, w_b, w_o, g, b, tm):
    n = x2d.shape[0]

    def rows(w):
        return pl.BlockSpec((tm, w), lambda i: (i, 0))

    return pl.pallas_call(
        functools.partial(_merge_kernel, alpha),
        grid=(n // tm,),
        in_specs=[rows(D_MODEL), rows(D_RNN), rows(P_Q), rows(D_MODEL), rows(D_MODEL),
                  _resident((D_RNN, D_MODEL)), _resident((P_Q, D_MODEL)), _resident((D_MODEL, D_MODEL)),
                  _resident((1, D_MODEL)), _resident((1, D_MODEL))],
        out_specs=rows(D_MODEL),
        out_shape=jax.ShapeDtypeStruct((n, D_MODEL), jnp.float32),
        compiler_params=_cparams(("parallel",)),
        name="merge",
    )(x2d, hg, o, ga, gb, w_a, w_b, w_o, g, b)


def _mlp_kernel(alpha, x_ref, wu_ref, bu_ref, wd_ref, bd_ref, g_ref, b_ref, y_ref):
    x = x_ref[...]
    h = jnp.dot(x.astype(jnp.bfloat16), wu_ref[...], preferred_element_type=jnp.float32) + bu_ref[...]
    h = jnp.maximum(h, 0.0)
    h = (h * h).astype(jnp.bfloat16)
    z = jnp.dot(h, wd_ref[...], preferred_element_type=jnp.float32) + bd_ref[...]
    y_ref[...] = _layer_norm(alpha * x + z, g_ref[...], b_ref[...])


def _mlp(alpha, x2d, w_up, b_up, w_down, b_down, g, b, tm):
    n = x2d.shape[0]
    return pl.pallas_call(
        functools.partial(_mlp_kernel, alpha),
        grid=(n // tm,),
        in_specs=[pl.BlockSpec((tm, D_MODEL), lambda i: (i, 0)),
                  _resident((D_MODEL, D_FF)), _resident((1, D_FF)),
                  _resident((D_FF, D_MODEL)), _resident((1, D_MODEL)),
                  _resident((1, D_MODEL)), _resident((1, D_MODEL))],
        out_specs=pl.BlockSpec((tm, D_MODEL), lambda i: (i, 0)),
        out_shape=jax.ShapeDtypeStruct((n, D_MODEL), jnp.float32),
        compiler_params=_cparams(("parallel",)),
        name="mlp",
    )(x2d, w_up, b_up, w_down, b_down, g, b)


def _pad_in_proj(w_in):
    splits = (D_RNN, D_RNN, N_HEADS * HEAD_DIM, KV_RANK, IDX_HEADS * IDX_DIM, IDX_DIM, IDX_HEADS,
              D_MODEL, D_MODEL)
    padded = (P_LRU, P_G, P_Q, P_CKV, P_QIDX, P_KIDX, P_WIDX, P_GA, P_GB)
    parts, off = [], 0
    for w, pw in zip(splits, padded):
        part = w_in[:, off:off + w]
        if pw > w:
            part = jnp.pad(part, ((0, 0), (0, pw - w)))
        parts.append(part)
        off += w
    return jnp.concatenate(parts, axis=1).astype(jnp.bfloat16)


def kernel(x, meta_tokens, ln_in_g, ln_in_b, w_in, conv_w, conv_b, w_rg_a, b_rg_a, w_rg_x, b_rg_x, lru_lambda,
           kv_norm_g, w_uk, w_uv, w_branch_a, w_branch_b, w_out, ln1_g, ln1_b, w_up, b_up, w_down, b_down,
           ln2_g, ln2_b):
    B, S, _ = x.shape
    depth = w_in.shape[0]
    assert S % ROW_TILE == 0
    T = S + N_META
    k_top = min(TOPK_MAX, T // 4)
    alpha = (2.0 * depth) ** 0.25
    n_blk = S // ROW_TILE + 1
    tp = n_blk * ROW_TILE
    tk = -(-tp // KEY_CHUNK) * KEY_CHUNK
    n = B * tp
    tm_proj, tm = 320, 640
    assert n % tm_proj == 0 and n % tm == 0
    bf16 = jnp.bfloat16

    def vec(a):
        return a.reshape(1, -1)

    meta_pad = jnp.pad(meta_tokens.astype(x.dtype), ((PADF, 0), (0, 0)))
    h = jnp.concatenate([jnp.broadcast_to(meta_pad[None], (B, ROW_TILE, D_MODEL)), x], axis=1).reshape(n, D_MODEL)

    for l in range(depth):
        outs = _proj(h, _pad_in_proj(w_in[l]), vec(kv_norm_g[l]), vec(ln_in_g), vec(ln_in_b), tm_proj, l == 0)
        if l == 0:
            h, *outs = outs
        lru, g_lru, q, ckv, qidx, kidx, widx, ga, gb = outs

        hg = _recurrent(lru.reshape(B, tp, D_RNN), g_lru.reshape(B, tp, D_RNN), conv_w[l], vec(conv_b[l]),
                        w_rg_a[l].astype(bf16), vec(b_rg_a[l]), w_rg_x[l].astype(bf16), vec(b_rg_x[l]),
                        vec(lru_lambda[l]))

        key_pad = ((0, 0), (0, tk - tp), (0, 0))
        kidx_c = jnp.pad(kidx.reshape(B, tp, P_KIDX), key_pad).reshape(B, tk // KEY_CHUNK, KEY_CHUNK, P_KIDX)
        ckv_c = jnp.pad(ckv.reshape(B, tp, P_CKV), key_pad).reshape(B, tk // KEY_CHUNK, KEY_CHUNK, P_CKV)
        o = _sparse_attention(qidx.reshape(B, tp, P_QIDX), widx.reshape(B, tp, P_WIDX), q.reshape(B, tp, P_Q),
                              kidx_c, ckv_c, jnp.swapaxes(ckv_c, 2, 3),
                              w_uk[l].astype(bf16), w_uv[l].astype(bf16), k_top)

        h = _merge(alpha, h, hg.reshape(n, D_RNN), o.reshape(n, P_Q), ga, gb, w_branch_a[l].astype(bf16),
                   w_branch_b[l].astype(bf16), w_out[l].astype(bf16), vec(ln1_g[l]), vec(ln1_b[l]), tm)
        h = _mlp(alpha, h, w_up[l].astype(bf16), vec(b_up[l]), w_down[l].astype(bf16), vec(b_down[l]),
                 vec(ln2_g[l]), vec(ln2_b[l]), tm)

    return h.reshape(B, tp, D_MODEL)[:, ROW_TILE:, :]
```

```python
import functools
import math

import jax
import jax.numpy as jnp
from jax import lax
from jax.experimental import pallas as pl
from jax.experimental.pallas import tpu as pltpu

D_MODEL = 1024
N_META = 16
D_RNN = 1280
LRU_BLOCKS = 10
LRU_BLOCK = D_RNN // LRU_BLOCKS
CONV_WIDTH = 4
LRU_C = 8.0
N_HEADS = 8
HEAD_DIM = 128
KV_RANK = 256
IDX_HEADS = 8
IDX_DIM = 64
TOPK_MAX = 256
D_FF = 4 * D_MODEL
LN_EPS = 1e-5
NEG_INF = -1e30

LANES = 128
ROW_TILE = 128
PADF = ROW_TILE - N_META
KEY_CHUNK = 512
VMEM_LIMIT = 56 * 1024 * 1024

P_LRU, P_G, P_Q, P_CKV, P_QIDX, P_KIDX, P_WIDX, P_GA, P_GB = (
    D_RNN, D_RNN, N_HEADS * HEAD_DIM, KV_RANK, IDX_HEADS * IDX_DIM, LANES, LANES, D_MODEL, D_MODEL)
P_OFFS = []
_acc = 0
for _w in (P_LRU, P_G, P_Q, P_CKV, P_QIDX, P_KIDX, P_WIDX, P_GA, P_GB):
    P_OFFS.append(_acc)
    _acc += _w
P_TOTAL = _acc


def _cparams(sem):
    return pltpu.CompilerParams(dimension_semantics=sem, vmem_limit_bytes=VMEM_LIMIT)


def _resident(shape):
    nd = len(shape)
    return pl.BlockSpec(shape, lambda *_: (0,) * nd, pipeline_mode=pl.Buffered(1))


def _layer_norm(x, g, b):
    mu = jnp.mean(x, axis=-1, keepdims=True)
    d = x - mu
    var = jnp.mean(d * d, axis=-1, keepdims=True)
    return d * lax.rsqrt(var + LN_EPS) * g + b


def _sigmoid(x):
    return 0.5 * jnp.tanh(0.5 * x) + 0.5


def _recurrent_cols(x, gelu, cs, first, cw_ref, cb_ref, wa_ref, ba_ref, wx_ref, bx_ref, lam_ref, xe_sc, h_sc):
    tm, width = x.shape
    row = lax.broadcasted_iota(jnp.int32, (tm, 1), 0)
    real = jnp.logical_or(jnp.logical_not(first), row >= PADF)
    x = jnp.where(real, x, 0.0)
    xe_sc[8:8 + tm, cs] = x
    cw = cw_ref[:, cs]
    xc = (cw[3:4] * x + cw[2:3] * xe_sc[7:7 + tm, cs] + cw[1:2] * xe_sc[6:6 + tm, cs]
          + cw[0:1] * xe_sc[5:5 + tm, cs] + cb_ref[:, cs])
    xe_sc[0:8, cs] = xe_sc[tm:tm + 8, cs]

    xcb = xc.astype(jnp.bfloat16)
    z = -lam_ref[:, cs]
    softplus = jnp.maximum(z, 0.0) + jnp.log1p(jnp.exp(-jnp.abs(z)))
    a_parts, u_parts = [], []
    for j in range(width // LRU_BLOCK):
        n = cs.start // LRU_BLOCK + j
        sl = slice(j * LRU_BLOCK, (j + 1) * LRU_BLOCK)
        gl = slice(cs.start + j * LRU_BLOCK, cs.start + (j + 1) * LRU_BLOCK)
        xn = xcb[:, sl]
        r = _sigmoid(jnp.dot(xn, wa_ref[n], preferred_element_type=jnp.float32) + ba_ref[:, gl])
        i = _sigmoid(jnp.dot(xn, wx_ref[n], preferred_element_type=jnp.float32) + bx_ref[:, gl])
        log_a = -LRU_C * r * softplus[:, sl]
        a = jnp.exp(log_a)
        u = jnp.sqrt(-jnp.tanh(log_a) * (a * a + 1.0)) * (i * xc[:, sl])
        a_parts.append(a)
        u_parts.append(jnp.where(real, u, 0.0))
    a = jnp.concatenate(a_parts, axis=1)
    u = jnp.concatenate(u_parts, axis=1)

    groups = tm // 8
    a = a.reshape(groups, 8, width)
    u = u.reshape(groups, 8, width)
    sub = lax.broadcasted_iota(jnp.int32, (1, 8, 1), 1)
    for d in (1, 2, 4):
        later = sub >= d
        u = jnp.where(later, a * pltpu.roll(u, d, axis=1) + u, u)
        a = jnp.where(later, a * pltpu.roll(a, d, axis=1), a)
    h = h_sc[:, cs]
    out = []
    for gi in range(groups):
        hg = a[gi] * h + u[gi]
        out.append(hg)
        h = hg[7:8, :]
    h_sc[:, cs] = h
    return jnp.concatenate(out, axis=0) * gelu


REC_COLS = 2 * LRU_BLOCK


def _proj_kernel(normalize, tiles_per_batch, x_ref, w_ref, kvg_ref, lng_ref, lnb_ref, cw_ref, cb_ref, wa_ref,
                 ba_ref, wx_ref, bx_ref, lam_ref, *refs):
    if normalize:
        h_ref, *refs = refs
        h = _layer_norm(x_ref[...], lng_ref[...], lnb_ref[...])
        h_ref[...] = h
        xb = h.astype(jnp.bfloat16)
    else:
        xb = x_ref[...].astype(jnp.bfloat16)
    hg_ref, q_ref, ckv_ref, qidx_ref, kidx_ref, widx_ref, ga_ref, gb_ref, xe_sc, h_sc = refs

    def cols(i, lo, width):
        return jnp.dot(xb, w_ref[:, P_OFFS[i] + lo:P_OFFS[i] + lo + width], preferred_element_type=jnp.float32)

    first = lax.rem(pl.program_id(0), tiles_per_batch) == 0

    @pl.when(first)
    def _():
        xe_sc[0:8, :] = jnp.zeros((8, D_RNN), jnp.float32)
        h_sc[...] = jnp.zeros_like(h_sc)

    def normed_latents():
        c = cols(3, 0, P_CKV)
        ms = jnp.mean(c * c, axis=-1, keepdims=True)
        ckv_ref[...] = (c * lax.rsqrt(ms + LN_EPS) * kvg_ref[...]).astype(jnp.bfloat16)

    def index_keys():
        k = cols(5, 0, P_KIDX)
        lane = lax.broadcasted_iota(jnp.int32, k.shape, 1)
        mu = jnp.sum(k, axis=-1, keepdims=True) * (1.0 / IDX_DIM)
        d = jnp.where(lane < IDX_DIM, k - mu, 0.0)
        var = jnp.sum(d * d, axis=-1, keepdims=True) * (1.0 / IDX_DIM)
        kidx_ref[...] = (d * lax.rsqrt(var + LN_EPS)).astype(jnp.bfloat16)
        widx_ref[...] = cols(6, 0, P_WIDX) * (IDX_HEADS ** -0.5 * IDX_DIM ** -0.5)

    def plain(ref, i, lo, width, act=None):
        def run():
            y = cols(i, lo, width)
            ref[:, lo:lo + width] = (y if act is None else act(y)).astype(ref.dtype)
        return run

    half = D_MODEL // 2
    others = [plain(q_ref, 2, 0, half), plain(q_ref, 2, half, half), normed_latents, plain(qidx_ref, 4, 0, P_QIDX),
              index_keys, plain(ga_ref, 7, 0, half, _sigmoid), plain(ga_ref, 7, half, half, _sigmoid),
              plain(gb_ref, 8, 0, half, _sigmoid), plain(gb_ref, 8, half, half, _sigmoid)]
    n_slices = D_RNN // REC_COLS
    per_slice = -(-len(others) // n_slices)
    for s in range(n_slices):
        cs = slice(s * REC_COLS, (s + 1) * REC_COLS)
        g = cols(1, cs.start, REC_COLS)
        gelu = 0.5 * g * (1.0 + jnp.tanh(math.sqrt(2.0 / math.pi) * (g + 0.044715 * (g * g * g))))
        hg_ref[:, cs] = _recurrent_cols(cols(0, cs.start, REC_COLS), gelu, cs, first, cw_ref, cb_ref, wa_ref, ba_ref,
                                        wx_ref, bx_ref, lam_ref, xe_sc, h_sc).astype(jnp.bfloat16)
        for run in others[s * per_slice:(s + 1) * per_slice]:
            run()


def _proj(x2d, w_in_p, kv_g, ln_g, ln_b, rec, tm, tiles_per_batch, normalize):
    n = x2d.shape[0]
    f32, bf16 = jnp.float32, jnp.bfloat16
    widths = (D_RNN, P_Q, P_CKV, P_QIDX, P_KIDX, P_WIDX, P_GA, P_GB)
    dtypes = (bf16, bf16, bf16, bf16, bf16, f32, f32, f32)
    if normalize:
        widths, dtypes = (D_MODEL,) + widths, (f32,) + dtypes
    return pl.pallas_call(
        functools.partial(_proj_kernel, normalize, tiles_per_batch),
        grid=(n // tm,),
        in_specs=[
            pl.BlockSpec((tm, D_MODEL), lambda i: (i, 0)),
            _resident((D_MODEL, P_TOTAL)),
            _resident((1, KV_RANK)),
            _resident((1, D_MODEL)),
            _resident((1, D_MODEL)),
            _resident((CONV_WIDTH, D_RNN)),
            _resident((1, D_RNN)),
            _resident((LRU_BLOCKS, LRU_BLOCK, LRU_BLOCK)),
            _resident((1, D_RNN)),
            _resident((LRU_BLOCKS, LRU_BLOCK, LRU_BLOCK)),
            _resident((1, D_RNN)),
            _resident((1, D_RNN)),
        ],
        out_specs=[pl.BlockSpec((tm, w), lambda i: (i, 0)) for w in widths],
        out_shape=[jax.ShapeDtypeStruct((n, w), dt) for w, dt in zip(widths, dtypes)],
        scratch_shapes=[
            pltpu.VMEM((tm + 8, D_RNN), jnp.float32),
            pltpu.VMEM((1, D_RNN), jnp.float32),
        ],
        compiler_params=_cparams(("arbitrary",)),
        name="proj",
    )(x2d, w_in_p, kv_g, ln_g, ln_b, *rec)


N_COLS = N_HEADS * ROW_TILE
COL_TILE = 2 * ROW_TILE
LOWEST = -3.0e38


def _ordered_bits_to_f32(k):
    return pltpu.bitcast(k ^ (lax.shift_right_arithmetic(k, 31) & 0x7FFFFFFF), jnp.float32)


def _attn_kernel(k_top, n_blocks, qi_ref, wi_ref, q_ref, kidx_ref, ckv_ref, ckvt_ref, wuk_ref, wuv_ref, triu_ref,
                 o_ref, sc_sc, sc16_sc, qit_sc, qabst_sc, acct_sc, m_sc, l_sc, alpha_sc, p_sc, s0_sc, thr_sc,
                 excess_sc):
    t = pl.program_id(1)
    n_chunks = sc_sc.shape[1]
    chunk_blocks = KEY_CHUNK // ROW_TILE
    blk_s = jnp.minimum(t, n_blocks - 1)
    nch_s = blk_s // chunk_blocks + 1
    nch_a = jnp.where(t > 0, (t - 1) // chunk_blocks + 1, 0)
    slot_s = lax.rem(t, 2)
    slot_a = 1 - slot_s
    q_pos = blk_s * ROW_TILE + lax.broadcasted_iota(jnp.int32, (1, ROW_TILE), 1)
    k_iota = lax.broadcasted_iota(jnp.int32, (KEY_CHUNK, 1), 0)

    qit = qi_ref[0].astype(jnp.float32).T
    for h in range(IDX_HEADS):
        qit_sc[:, h * ROW_TILE:(h + 1) * ROW_TILE] = qit[h * IDX_DIM:(h + 1) * IDX_DIM, :].astype(jnp.bfloat16)
    wit = wi_ref[0].T
    wi_row = jnp.concatenate([wit[h:h + 1, :] for h in range(IDX_HEADS)], axis=1)
    qt = q_ref[0].astype(jnp.float32).T
    scale = HEAD_DIM ** -0.5 * math.log2(math.e)
    for h in range(N_HEADS):
        qa = jnp.dot(wuk_ref[h], qt[h * HEAD_DIM:(h + 1) * HEAD_DIM, :].astype(jnp.bfloat16),
                     preferred_element_type=jnp.float32)
        qabst_sc[:, h * ROW_TILE:(h + 1) * ROW_TILE] = (qa * scale).astype(jnp.bfloat16)

    def score_chunk(c):
        ks = kidx_ref[0, c][:, 0:IDX_DIM]
        sc = None
        for j in range(N_COLS // COL_TILE):
            cols = slice(j * COL_TILE, (j + 1) * COL_TILE)
            lg = jnp.dot(ks, qit_sc[:, cols], preferred_element_type=jnp.float32)
            w = jnp.maximum(lg, 0.0) * wi_row[:, cols]
            part = w[:, :ROW_TILE] + w[:, ROW_TILE:]
            sc = part if sc is None else sc + part
        k_pos = c * KEY_CHUNK + k_iota
        visible = jnp.logical_and(k_pos <= q_pos, k_pos >= PADF)
        sc = jnp.where(visible, sc, -jnp.inf)
        sc_sc[slot_s, c] = sc
        sc16_sc[slot_s, c] = sc.astype(jnp.bfloat16)

    @pl.when(t == 0)
    def _():
        thr_sc[...] = jnp.full_like(thr_sc, LOWEST)

    thr_a = thr_sc[...]
    n_tiles = N_COLS // COL_TILE
    last = slice((n_tiles - 1) * COL_TILE, n_tiles * COL_TILE)
    m_sc[...] = jnp.full_like(m_sc, NEG_INF)
    l_sc[...] = jnp.zeros_like(l_sc)
    acct_sc[...] = jnp.zeros_like(acct_sc)
    alpha_sc[:, last] = jnp.ones((1, COL_TILE), jnp.float32)
    p_sc[:, last] = jnp.zeros((KEY_CHUNK, COL_TILE), jnp.bfloat16)

    def logits_tile(kc, j):
        cols = slice(j * COL_TILE, (j + 1) * COL_TILE)
        return jnp.dot(kc, qabst_sc[:, cols], preferred_element_type=jnp.float32)

    def softmax_tile(s, sel, j):
        cols = slice(j * COL_TILE, (j + 1) * COL_TILE)
        s = jnp.concatenate([jnp.where(sel, s[:, :ROW_TILE], NEG_INF),
                             jnp.where(sel, s[:, ROW_TILE:], NEG_INF)], axis=1)
        m_prev = m_sc[:, cols]
        m_new = jnp.maximum(m_prev, jnp.max(s, axis=0, keepdims=True))
        p = jnp.exp2(s - m_new)
        alpha = jnp.exp2(m_prev - m_new)
        l_sc[:, cols] = alpha * l_sc[:, cols] + jnp.sum(p, axis=0, keepdims=True)
        m_sc[:, cols] = m_new
        alpha_sc[:, cols] = alpha
        p_sc[:, cols] = p.astype(jnp.bfloat16)

    def value_tile(kct, j):
        cols = slice(j * COL_TILE, (j + 1) * COL_TILE)
        acct_sc[:, cols] = alpha_sc[:, cols] * acct_sc[:, cols] + jnp.dot(
            kct, p_sc[:, cols], preferred_element_type=jnp.float32)

    def attn_chunk(c):
        kc = ckv_ref[0, c]
        kct = ckvt_ref[0, c]
        sel = sc_sc[slot_a, c] >= thr_a
        softmax_tile(s0_sc[...], sel, 0)
        value_tile(ckvt_ref[0, jnp.maximum(c - 1, 0)], n_tiles - 1)
        for j in range(1, n_tiles):
            softmax_tile(logits_tile(kc, j), sel, j)
            value_tile(kct, j - 1)
        s0_sc[...] = logits_tile(ckv_ref[0, jnp.minimum(c + 1, n_chunks - 1)], 0)

    def both(c, carry):
        attn_chunk(c)
        score_chunk(c)
        return carry

    def score_only(c, carry):
        score_chunk(c)
        return carry

    s0_sc[...] = logits_tile(ckv_ref[0, 0], 0)
    lax.fori_loop(0, nch_a, both, 0)
    lax.fori_loop(nch_a, nch_s, score_only, 0)

    @pl.when(t > 0)
    def _():
        value_tile(ckvt_ref[0, nch_a - 1], n_tiles - 1)
        o_lat_t = acct_sc[...] / l_sc[...]
        for h in range(N_HEADS):
            o_lat = o_lat_t[:, h * ROW_TILE:(h + 1) * ROW_TILE].T.astype(jnp.bfloat16)
            oh = jnp.dot(o_lat, wuv_ref[h], preferred_element_type=jnp.float32)
            o_ref[0, :, h * HEAD_DIM:(h + 1) * HEAD_DIM] = oh.astype(jnp.bfloat16)

    def threshold_search(n):
        def count_ge(cand):
            acc = jnp.zeros((8, ROW_TILE), jnp.int32)
            for c in range(n):
                hit = jnp.where(sc_sc[slot_s, c] >= cand, 1, 0)
                acc = acc + jnp.sum(hit.reshape(KEY_CHUNK // 8, 8, ROW_TILE), axis=0)
            return jnp.sum(acc, axis=0, keepdims=True)

        def count16_ge(cand):
            one = jnp.ones((), jnp.bfloat16)
            zero = jnp.zeros((), jnp.bfloat16)
            acc = jnp.zeros((16, ROW_TILE), jnp.float32)
            for c in range(n):
                hit = jnp.where(sc16_sc[slot_s, c] >= cand, one, zero).reshape(KEY_CHUNK // 16, 16, ROW_TILE)
                parts = [hit[i] for i in range(KEY_CHUNK // 16)]
                while len(parts) > 1:
                    parts = [parts[i] + parts[i + 1] for i in range(0, len(parts), 2)]
                acc = acc + parts[0].astype(jnp.float32)
            return jnp.sum(acc, axis=0, keepdims=True)

        def coarse_pass(i, base):
            cand = base + lax.shift_left(jnp.int32(1), 15 - i)
            raw = cand ^ (lax.shift_right_arithmetic(cand, 15) & 0x7FFF)
            cand_f = pltpu.bitcast(lax.shift_left(raw, 16), jnp.float32)
            cnt = count16_ge(cand_f.astype(jnp.bfloat16))
            return jnp.where(cnt >= k_top, cand, base)

        v16 = lax.fori_loop(0, 16, coarse_pass, jnp.full((1, ROW_TILE), -(2 ** 15), jnp.int32))
        few = v16 == -(2 ** 15)

        centre = lax.shift_left(v16, 16) + jnp.where(v16 < 0, 2 ** 16 - 1, 0)
        lo_k = centre - (2 ** 15 + 1)
        span = 3 * 2 ** 15 + 1

        def fine_pass(i, state):
            off, n_ge = state
            cand_off = off + lax.shift_left(jnp.int32(1), 16 - i)
            cnt = count_ge(_ordered_bits_to_f32(lo_k + cand_off))
            keep = jnp.logical_and(cnt >= k_top, cand_off < span)
            return jnp.where(keep, cand_off, off), jnp.where(keep, cnt, n_ge)

        off, n_ge = lax.fori_loop(0, 17, fine_pass, (jnp.zeros((1, ROW_TILE), jnp.int32),
                                                     jnp.zeros((1, ROW_TILE), jnp.int32)))
        thr_sc[...] = jnp.where(few, LOWEST, _ordered_bits_to_f32(lo_k + off))
        excess_sc[...] = jnp.where(few, 0, jnp.maximum(n_ge - k_top, 0))

    for n in range(1, n_chunks + 1):
        pl.when(jnp.logical_and(nch_s == n, t < n_blocks))(functools.partial(threshold_search, n))
    thr = thr_sc[...]
    excess = excess_sc[...]

    @pl.when(jnp.logical_and(jnp.max(excess) > 0, t < n_blocks))
    def _():
        surplus = excess.astype(jnp.float32)

        def demote(i, after):
            c = nch_s - 1 - i
            s = sc_sc[slot_s, c]
            tied = s == thr
            rank = after + jnp.dot(triu_ref[...], jnp.where(tied, 1.0, 0.0).astype(jnp.bfloat16),
                                   preferred_element_type=jnp.float32)
            sc_sc[slot_s, c] = jnp.where(jnp.logical_and(tied, rank <= surplus), -jnp.inf, s)
            return rank[0:1, :]

        lax.fori_loop(0, nch_s, demote, jnp.zeros((1, ROW_TILE), jnp.float32))


def _sparse_attention(qidx, widx, q, kidx_c, ckv_c, ckvt_c, wuk, wuv, k_top):
    B, tp, _ = q.shape
    nch = kidx_c.shape[1]
    nblk = tp // ROW_TILE

    def scored(bi, t):
        return (bi, jnp.minimum(t, nblk - 1), 0)

    def attended(bi, t):
        return (bi, jnp.maximum(t - 1, 0), 0)

    return pl.pallas_call(
        functools.partial(_attn_kernel, k_top, nblk),
        grid=(B, nblk + 1),
        in_specs=[
            pl.BlockSpec((1, ROW_TILE, P_QIDX), scored),
            pl.BlockSpec((1, ROW_TILE, P_WIDX), scored),
            pl.BlockSpec((1, ROW_TILE, P_Q), attended),
            pl.BlockSpec((1, nch, KEY_CHUNK, P_KIDX), lambda bi, t: (bi, 0, 0, 0)),
            pl.BlockSpec((1, nch, KEY_CHUNK, P_CKV), lambda bi, t: (bi, 0, 0, 0)),
            pl.BlockSpec((1, nch, P_CKV, KEY_CHUNK), lambda bi, t: (bi, 0, 0, 0)),
            _resident((N_HEADS, KV_RANK, HEAD_DIM)),
            _resident((N_HEADS, KV_RANK, HEAD_DIM)),
            _resident((KEY_CHUNK, KEY_CHUNK)),
        ],
        out_specs=pl.BlockSpec((1, ROW_TILE, P_Q), attended),
        out_shape=jax.ShapeDtypeStruct((B, tp, P_Q), jnp.bfloat16),
        scratch_shapes=[
            pltpu.VMEM((2, nch, KEY_CHUNK, ROW_TILE), jnp.float32),
            pltpu.VMEM((2, nch, KEY_CHUNK, ROW_TILE), jnp.bfloat16),
            pltpu.VMEM((IDX_DIM, N_COLS), jnp.bfloat16),
            pltpu.VMEM((KV_RANK, N_COLS), jnp.bfloat16),
            pltpu.VMEM((KV_RANK, N_COLS), jnp.float32),
            pltpu.VMEM((1, N_COLS), jnp.float32),
            pltpu.VMEM((1, N_COLS), jnp.float32),
            pltpu.VMEM((1, N_COLS), jnp.float32),
            pltpu.VMEM((KEY_CHUNK, N_COLS), jnp.bfloat16),
            pltpu.VMEM((KEY_CHUNK, COL_TILE), jnp.float32),
            pltpu.VMEM((1, ROW_TILE), jnp.float32),
            pltpu.VMEM((1, ROW_TILE), jnp.int32),
        ],
        compiler_params=_cparams(("parallel", "arbitrary")),
        name="sparse_attn",
    )(qidx, widx, q, kidx_c, ckv_c, ckvt_c, wuk, wuv, jnp.triu(jnp.ones((KEY_CHUNK, KEY_CHUNK), jnp.bfloat16)))


def _merge_kernel(alpha, x_ref, hg_ref, o_ref, ga_ref, gb_ref, wa_ref, wb_ref, wo_ref, g_ref, b_ref, y_ref):
    half = x_ref.shape[0] // 2
    for r in (slice(0, half), slice(half, 2 * half)):
        ya = jnp.dot(hg_ref[r, :], wa_ref[...], preferred_element_type=jnp.float32)
        yb = jnp.dot(o_ref[r, :], wb_ref[...], preferred_element_type=jnp.float32)
        mixed = ga_ref[r, :] * ya + gb_ref[r, :] * yb
        z = jnp.dot(mixed.astype(jnp.bfloat16), wo_ref[...], preferred_element_type=jnp.float32)
        y_ref[r, :] = _layer_norm(alpha * x_ref[r, :] + z, g_ref[...], b_ref[...])


def _merge(alpha, x2d, hg, o, ga, gb, w_a, w_b, w_o, g, b, tm):
    n = x2d.shape[0]

    def rows(w):
        return pl.BlockSpec((tm, w), lambda i: (i, 0))

    return pl.pallas_call(
        functools.partial(_merge_kernel, alpha),
        grid=(n // tm,),
        in_specs=[rows(D_MODEL), rows(D_RNN), rows(P_Q), rows(D_MODEL), rows(D_MODEL),
                  _resident((D_RNN, D_MODEL)), _resident((P_Q, D_MODEL)), _resident((D_MODEL, D_MODEL)),
                  _resident((1, D_MODEL)), _resident((1, D_MODEL))],
        out_specs=rows(D_MODEL),
        out_shape=jax.ShapeDtypeStruct((n, D_MODEL), jnp.float32),
        compiler_params=_cparams(("parallel",)),
        name="merge",
    )(x2d, hg, o, ga, gb, w_a, w_b, w_o, g, b)


def _mlp_kernel(alpha, x_ref, wu_ref, bu_ref, wd_ref, bd_ref, g_ref, b_ref, y_ref):
    x = x_ref[...]
    h = jnp.dot(x.astype(jnp.bfloat16), wu_ref[...], preferred_element_type=jnp.float32) + bu_ref[...]
    h = jnp.maximum(h, 0.0)
    h = (h * h).astype(jnp.bfloat16)
    z = jnp.dot(h, wd_ref[...], preferred_element_type=jnp.float32) + bd_ref[...]
    y_ref[...] = _layer_norm(alpha * x + z, g_ref[...], b_ref[...])


def _mlp(alpha, x2d, w_up, b_up, w_down, b_down, g, b, tm):
    n = x2d.shape[0]
    return pl.pallas_call(
        functools.partial(_mlp_kernel, alpha),
        grid=(n // tm,),
        in_specs=[pl.BlockSpec((tm, D_MODEL), lambda i: (i, 0)),
                  _resident((D_MODEL, D_FF)), _resident((1, D_FF)),
                  _resident((D_FF, D_MODEL)), _resident((1, D_MODEL)),
                  _resident((1, D_MODEL)), _resident((1, D_MODEL))],
        out_specs=pl.BlockSpec((tm, D_MODEL), lambda i: (i, 0)),
        out_shape=jax.ShapeDtypeStruct((n, D_MODEL), jnp.float32),
        compiler_params=_cparams(("parallel",)),
        name="mlp",
    )(x2d, w_up, b_up, w_down, b_down, g, b)


def _pad_in_proj(w_in):
    splits = (D_RNN, D_RNN, N_HEADS * HEAD_DIM, KV_RANK, IDX_HEADS * IDX_DIM, IDX_DIM, IDX_HEADS,
              D_MODEL, D_MODEL)
    padded = (P_LRU, P_G, P_Q, P_CKV, P_QIDX, P_KIDX, P_WIDX, P_GA, P_GB)
    parts, off = [], 0
    for w, pw in zip(splits, padded):
        part = w_in[:, off:off + w]
        if pw > w:
            part = jnp.pad(part, ((0, 0), (0, pw - w)))
        parts.append(part)
        off += w
    return jnp.concatenate(parts, axis=1).astype(jnp.bfloat16)


def kernel(x, meta_tokens, ln_in_g, ln_in_b, w_in, conv_w, conv_b, w_rg_a, b_rg_a, w_rg_x, b_rg_x, lru_lambda,
           kv_norm_g, w_uk, w_uv, w_branch_a, w_branch_b, w_out, ln1_g, ln1_b, w_up, b_up, w_down, b_down,
           ln2_g, ln2_b):
    B, S, _ = x.shape
    depth = w_in.shape[0]
    assert S % ROW_TILE == 0
    T = S + N_META
    k_top = min(TOPK_MAX, T // 4)
    alpha = (2.0 * depth) ** 0.25
    n_blk = S // ROW_TILE + 1
    tp = n_blk * ROW_TILE
    tk = -(-tp // KEY_CHUNK) * KEY_CHUNK
    n = B * tp
    tm_proj, tm = 320, 640
    assert tp % tm_proj == 0 and n % tm == 0
    bf16 = jnp.bfloat16

    def vec(a):
        return a.reshape(1, -1)

    meta_pad = jnp.pad(meta_tokens.astype(x.dtype), ((PADF, 0), (0, 0)))
    h = jnp.concatenate([jnp.broadcast_to(meta_pad[None], (B, ROW_TILE, D_MODEL)), x], axis=1).reshape(n, D_MODEL)

    for l in range(depth):
        rec = (conv_w[l], vec(conv_b[l]), w_rg_a[l].astype(bf16), vec(b_rg_a[l]), w_rg_x[l].astype(bf16),
               vec(b_rg_x[l]), vec(lru_lambda[l]))
        outs = _proj(h, _pad_in_proj(w_in[l]), vec(kv_norm_g[l]), vec(ln_in_g), vec(ln_in_b), rec, tm_proj,
                     tp // tm_proj, l == 0)
        if l == 0:
            h, *outs = outs
        hg, q, ckv, qidx, kidx, widx, ga, gb = outs

        key_pad = ((0, 0), (0, tk - tp), (0, 0))
        kidx_c = jnp.pad(kidx.reshape(B, tp, P_KIDX), key_pad).reshape(B, tk // KEY_CHUNK, KEY_CHUNK, P_KIDX)
        ckv_c = jnp.pad(ckv.reshape(B, tp, P_CKV), key_pad).reshape(B, tk // KEY_CHUNK, KEY_CHUNK, P_CKV)
        o = _sparse_attention(qidx.reshape(B, tp, P_QIDX), widx.reshape(B, tp, P_WIDX), q.reshape(B, tp, P_Q),
                              kidx_c, ckv_c, jnp.swapaxes(ckv_c, 2, 3),
                              w_uk[l].astype(bf16), w_uv[l].astype(bf16), k_top)

        h = _merge(alpha, h, hg, o.reshape(n, P_Q), ga, gb, w_branch_a[l].astype(bf16),
                   w_branch_b[l].astype(bf16), w_out[l].astype(bf16), vec(ln1_g[l]), vec(ln1_b[l]), tm)
        h = _mlp(alpha, h, w_up[l].astype(bf16), vec(b_up[l]), w_down[l].astype(bf16), vec(b_down[l]),
                 vec(ln2_g[l]), vec(ln2_b[l]), tm)

    return h.reshape(B, tp, D_MODEL)[:, ROW_TILE:, :]
```

```python
import functools
import math

import jax
import jax.numpy as jnp
from jax import lax
from jax.experimental import pallas as pl
from jax.experimental.pallas import tpu as pltpu

D_MODEL = 1024
N_META = 16
D_RNN = 1280
LRU_BLOCKS = 10
LRU_BLOCK = D_RNN // LRU_BLOCKS
CONV_WIDTH = 4
LRU_C = 8.0
N_HEADS = 8
HEAD_DIM = 128
KV_RANK = 256
IDX_HEADS = 8
IDX_DIM = 64
TOPK_MAX = 256
D_FF = 4 * D_MODEL
LN_EPS = 1e-5
NEG_INF = -1e30

LANES = 128
ROW_TILE = 128
PADF = ROW_TILE - N_META
KEY_CHUNK = 512
VMEM_LIMIT = 56 * 1024 * 1024

P_LRU, P_G, P_Q, P_CKV, P_QIDX, P_KIDX, P_WIDX, P_GA, P_GB = (
    D_RNN, D_RNN, N_HEADS * HEAD_DIM, KV_RANK, IDX_HEADS * IDX_DIM, LANES, LANES, D_MODEL, D_MODEL)
P_OFFS = []
_acc = 0
for _w in (P_LRU, P_G, P_Q, P_CKV, P_QIDX, P_KIDX, P_WIDX, P_GA, P_GB):
    P_OFFS.append(_acc)
    _acc += _w
P_TOTAL = _acc


def _cparams(sem):
    return pltpu.CompilerParams(dimension_semantics=sem, vmem_limit_bytes=VMEM_LIMIT)


def _resident(shape):
    nd = len(shape)
    return pl.BlockSpec(shape, lambda *_: (0,) * nd, pipeline_mode=pl.Buffered(1))


def _layer_norm(x, g, b):
    mu = jnp.mean(x, axis=-1, keepdims=True)
    d = x - mu
    var = jnp.mean(d * d, axis=-1, keepdims=True)
    return d * lax.rsqrt(var + LN_EPS) * g + b


def _sigmoid(x):
    return 0.5 * jnp.tanh(0.5 * x) + 0.5


def _recurrent_cols(x, gelu, cs, first, cw_ref, cb_ref, wa_ref, ba_ref, wx_ref, bx_ref, lam_ref, xe_sc, h_sc):
    tm, width = x.shape
    row = lax.broadcasted_iota(jnp.int32, (tm, 1), 0)
    real = jnp.logical_or(jnp.logical_not(first), row >= PADF)
    x = jnp.where(real, x, 0.0)
    xe_sc[8:8 + tm, cs] = x
    cw = cw_ref[:, cs]
    xc = (cw[3:4] * x + cw[2:3] * xe_sc[7:7 + tm, cs] + cw[1:2] * xe_sc[6:6 + tm, cs]
          + cw[0:1] * xe_sc[5:5 + tm, cs] + cb_ref[:, cs])
    xe_sc[0:8, cs] = xe_sc[tm:tm + 8, cs]

    xcb = xc.astype(jnp.bfloat16)
    z = -lam_ref[:, cs]
    softplus = jnp.maximum(z, 0.0) + jnp.log1p(jnp.exp(-jnp.abs(z)))
    a_parts, u_parts = [], []
    for j in range(width // LRU_BLOCK):
        n = cs.start // LRU_BLOCK + j
        sl = slice(j * LRU_BLOCK, (j + 1) * LRU_BLOCK)
        gl = slice(cs.start + j * LRU_BLOCK, cs.start + (j + 1) * LRU_BLOCK)
        xn = xcb[:, sl]
        r = _sigmoid(jnp.dot(xn, wa_ref[n], preferred_element_type=jnp.float32) + ba_ref[:, gl])
        i = _sigmoid(jnp.dot(xn, wx_ref[n], preferred_element_type=jnp.float32) + bx_ref[:, gl])
        log_a = -LRU_C * r * softplus[:, sl]
        a = jnp.exp(log_a)
        u = jnp.sqrt(-jnp.tanh(log_a) * (a * a + 1.0)) * (i * xc[:, sl])
        a_parts.append(a)
        u_parts.append(jnp.where(real, u, 0.0))
    a = jnp.concatenate(a_parts, axis=1)
    u = jnp.concatenate(u_parts, axis=1)

    groups = tm // 8
    a = a.reshape(groups, 8, width)
    u = u.reshape(groups, 8, width)
    sub = lax.broadcasted_iota(jnp.int32, (1, 8, 1), 1)
    for d in (1, 2, 4):
        later = sub >= d
        u = jnp.where(later, a * pltpu.roll(u, d, axis=1) + u, u)
        a = jnp.where(later, a * pltpu.roll(a, d, axis=1), a)
    h = h_sc[:, cs]
    out = []
    for gi in range(groups):
        hg = a[gi] * h + u[gi]
        out.append(hg)
        h = hg[7:8, :]
    h_sc[:, cs] = h
    return jnp.concatenate(out, axis=0) * gelu


REC_COLS = 2 * LRU_BLOCK


def _proj_kernel(normalize, tiles_per_batch, x_ref, w_ref, kvg_ref, lng_ref, lnb_ref, cw_ref, cb_ref, wa_ref,
                 ba_ref, wx_ref, bx_ref, lam_ref, *refs):
    if normalize:
        h_ref, *refs = refs
        h = _layer_norm(x_ref[...], lng_ref[...], lnb_ref[...])
        h_ref[...] = h
        xb = h.astype(jnp.bfloat16)
    else:
        xb = x_ref[...].astype(jnp.bfloat16)
    hg_ref, q_ref, ckv_ref, qidx_ref, kidx_ref, widx_ref, ga_ref, gb_ref, xe_sc, h_sc = refs

    def cols(i, lo, width):
        return jnp.dot(xb, w_ref[:, P_OFFS[i] + lo:P_OFFS[i] + lo + width], preferred_element_type=jnp.float32)

    first = lax.rem(pl.program_id(0), tiles_per_batch) == 0

    @pl.when(first)
    def _():
        xe_sc[0:8, :] = jnp.zeros((8, D_RNN), jnp.float32)
        h_sc[...] = jnp.zeros_like(h_sc)

    def normed_latents():
        c = cols(3, 0, P_CKV)
        ms = jnp.mean(c * c, axis=-1, keepdims=True)
        ckv_ref[...] = (c * lax.rsqrt(ms + LN_EPS) * kvg_ref[...]).astype(jnp.bfloat16)

    def index_keys():
        k = cols(5, 0, P_KIDX)
        lane = lax.broadcasted_iota(jnp.int32, k.shape, 1)
        mu = jnp.sum(k, axis=-1, keepdims=True) * (1.0 / IDX_DIM)
        d = jnp.where(lane < IDX_DIM, k - mu, 0.0)
        var = jnp.sum(d * d, axis=-1, keepdims=True) * (1.0 / IDX_DIM)
        kidx_ref[...] = (d * lax.rsqrt(var + LN_EPS)).astype(jnp.bfloat16)
        widx_ref[...] = cols(6, 0, P_WIDX) * (IDX_HEADS ** -0.5 * IDX_DIM ** -0.5)

    def plain(ref, i, lo, width, act=None):
        def run():
            y = cols(i, lo, width)
            ref[:, lo:lo + width] = (y if act is None else act(y)).astype(ref.dtype)
        return run

    half = D_MODEL // 2
    others = [plain(q_ref, 2, 0, half), plain(q_ref, 2, half, half), normed_latents, plain(qidx_ref, 4, 0, P_QIDX),
              index_keys, plain(ga_ref, 7, 0, half, _sigmoid), plain(ga_ref, 7, half, half, _sigmoid),
              plain(gb_ref, 8, 0, half, _sigmoid), plain(gb_ref, 8, half, half, _sigmoid)]
    n_slices = D_RNN // REC_COLS
    per_slice = -(-len(others) // n_slices)
    for s in range(n_slices):
        cs = slice(s * REC_COLS, (s + 1) * REC_COLS)
        g = cols(1, cs.start, REC_COLS)
        gelu = 0.5 * g * (1.0 + jnp.tanh(math.sqrt(2.0 / math.pi) * (g + 0.044715 * (g * g * g))))
        hg_ref[:, cs] = _recurrent_cols(cols(0, cs.start, REC_COLS), gelu, cs, first, cw_ref, cb_ref, wa_ref, ba_ref,
                                        wx_ref, bx_ref, lam_ref, xe_sc, h_sc).astype(jnp.bfloat16)
        for run in others[s * per_slice:(s + 1) * per_slice]:
            run()


def _proj(x2d, w_in_p, kv_g, ln_g, ln_b, rec, tm, tiles_per_batch, normalize):
    n = x2d.shape[0]
    f32, bf16 = jnp.float32, jnp.bfloat16
    widths = (D_RNN, P_Q, P_CKV, P_QIDX, P_KIDX, P_WIDX, P_GA, P_GB)
    dtypes = (bf16, bf16, bf16, bf16, bf16, f32, f32, f32)
    if normalize:
        widths, dtypes = (D_MODEL,) + widths, (f32,) + dtypes
    return pl.pallas_call(
        functools.partial(_proj_kernel, normalize, tiles_per_batch),
        grid=(n // tm,),
        in_specs=[
            pl.BlockSpec((tm, D_MODEL), lambda i: (i, 0)),
            _resident((D_MODEL, P_TOTAL)),
            _resident((1, KV_RANK)),
            _resident((1, D_MODEL)),
            _resident((1, D_MODEL)),
            _resident((CONV_WIDTH, D_RNN)),
            _resident((1, D_RNN)),
            _resident((LRU_BLOCKS, LRU_BLOCK, LRU_BLOCK)),
            _resident((1, D_RNN)),
            _resident((LRU_BLOCKS, LRU_BLOCK, LRU_BLOCK)),
            _resident((1, D_RNN)),
            _resident((1, D_RNN)),
        ],
        out_specs=[pl.BlockSpec((tm, w), lambda i: (i, 0)) for w in widths],
        out_shape=[jax.ShapeDtypeStruct((n, w), dt) for w, dt in zip(widths, dtypes)],
        scratch_shapes=[
            pltpu.VMEM((tm + 8, D_RNN), jnp.float32),
            pltpu.VMEM((1, D_RNN), jnp.float32),
        ],
        compiler_params=_cparams(("arbitrary",)),
        name="proj",
    )(x2d, w_in_p, kv_g, ln_g, ln_b, *rec)


N_COLS = N_HEADS * ROW_TILE
COL_TILE = 2 * ROW_TILE
LOWEST = -3.0e38


def _ordered_bits_to_f32(k):
    return pltpu.bitcast(k ^ (lax.shift_right_arithmetic(k, 31) & 0x7FFFFFFF), jnp.float32)


def _attn_kernel(k_top, n_blocks, qi_ref, wi_ref, q_ref, kidx_ref, ckv_ref, ckvt_ref, wuk_ref, wuv_ref, triu_ref,
                 o_ref, sc_sc, sc16_sc, qit_sc, qabst_sc, acct_sc, m_sc, l_sc, alpha_sc, p_sc, s0_sc, thr_sc,
                 excess_sc):
    t = pl.program_id(1)
    n_chunks = sc_sc.shape[1]
    chunk_blocks = KEY_CHUNK // ROW_TILE
    blk_s = jnp.minimum(t, n_blocks - 1)
    nch_s = blk_s // chunk_blocks + 1
    nch_a = jnp.where(t > 0, (t - 1) // chunk_blocks + 1, 0)
    slot_s = lax.rem(t, 2)
    slot_a = 1 - slot_s
    q_pos = blk_s * ROW_TILE + lax.broadcasted_iota(jnp.int32, (1, ROW_TILE), 1)
    k_iota = lax.broadcasted_iota(jnp.int32, (KEY_CHUNK, 1), 0)

    qit = qi_ref[0].astype(jnp.float32).T
    for h in range(IDX_HEADS):
        qit_sc[:, h * ROW_TILE:(h + 1) * ROW_TILE] = qit[h * IDX_DIM:(h + 1) * IDX_DIM, :].astype(jnp.bfloat16)
    wit = wi_ref[0].T
    wi_row = jnp.concatenate([wit[h:h + 1, :] for h in range(IDX_HEADS)], axis=1)
    qt = q_ref[0].astype(jnp.float32).T
    scale = HEAD_DIM ** -0.5 * math.log2(math.e)
    for h in range(N_HEADS):
        qa = jnp.dot(wuk_ref[h], qt[h * HEAD_DIM:(h + 1) * HEAD_DIM, :].astype(jnp.bfloat16),
                     preferred_element_type=jnp.float32)
        qabst_sc[:, h * ROW_TILE:(h + 1) * ROW_TILE] = (qa * scale).astype(jnp.bfloat16)

    def score_chunk(c):
        ks = kidx_ref[0, c][:, 0:IDX_DIM]
        sc = None
        for j in range(N_COLS // COL_TILE):
            cols = slice(j * COL_TILE, (j + 1) * COL_TILE)
            lg = jnp.dot(ks, qit_sc[:, cols], preferred_element_type=jnp.float32)
            w = jnp.maximum(lg, 0.0) * wi_row[:, cols]
            part = w[:, :ROW_TILE] + w[:, ROW_TILE:]
            sc = part if sc is None else sc + part
        k_pos = c * KEY_CHUNK + k_iota
        visible = jnp.logical_and(k_pos <= q_pos, k_pos >= PADF)
        sc = jnp.where(visible, sc, -jnp.inf)
        sc_sc[slot_s, c] = sc
        sc16_sc[slot_s, c] = sc.astype(jnp.bfloat16)

    @pl.when(t == 0)
    def _():
        thr_sc[...] = jnp.full_like(thr_sc, LOWEST)

    thr_a = thr_sc[...]
    n_tiles = N_COLS // COL_TILE
    last = slice((n_tiles - 1) * COL_TILE, n_tiles * COL_TILE)
    m_sc[...] = jnp.full_like(m_sc, NEG_INF)
    l_sc[...] = jnp.zeros_like(l_sc)
    acct_sc[...] = jnp.zeros_like(acct_sc)
    alpha_sc[:, last] = jnp.ones((1, COL_TILE), jnp.float32)
    p_sc[:, last] = jnp.zeros((KEY_CHUNK, COL_TILE), jnp.bfloat16)

    def logits_tile(kc, j):
        cols = slice(j * COL_TILE, (j + 1) * COL_TILE)
        return jnp.dot(kc, qabst_sc[:, cols], preferred_element_type=jnp.float32)

    def softmax_tile(s, sel, j):
        cols = slice(j * COL_TILE, (j + 1) * COL_TILE)
        s = jnp.concatenate([jnp.where(sel, s[:, :ROW_TILE], NEG_INF),
                             jnp.where(sel, s[:, ROW_TILE:], NEG_INF)], axis=1)
        m_prev = m_sc[:, cols]
        m_new = jnp.maximum(m_prev, jnp.max(s, axis=0, keepdims=True))
        p = jnp.exp2(s - m_new)
        alpha = jnp.exp2(m_prev - m_new)
        l_sc[:, cols] = alpha * l_sc[:, cols] + jnp.sum(p, axis=0, keepdims=True)
        m_sc[:, cols] = m_new
        alpha_sc[:, cols] = alpha
        p_sc[:, cols] = p.astype(jnp.bfloat16)

    def value_tile(kct, j):
        cols = slice(j * COL_TILE, (j + 1) * COL_TILE)
        acct_sc[:, cols] = alpha_sc[:, cols] * acct_sc[:, cols] + jnp.dot(
            kct, p_sc[:, cols], preferred_element_type=jnp.float32)

    def attn_chunk(c):
        kc = ckv_ref[0, c]
        kct = ckvt_ref[0, c]
        sel = sc_sc[slot_a, c] >= thr_a
        softmax_tile(s0_sc[...], sel, 0)
        value_tile(ckvt_ref[0, jnp.maximum(c - 1, 0)], n_tiles - 1)
        for j in range(1, n_tiles):
            softmax_tile(logits_tile(kc, j), sel, j)
            value_tile(kct, j - 1)
        s0_sc[...] = logits_tile(ckv_ref[0, jnp.minimum(c + 1, n_chunks - 1)], 0)

    def both(c, carry):
        attn_chunk(c)
        score_chunk(c)
        return carry

    def score_only(c, carry):
        score_chunk(c)
        return carry

    s0_sc[...] = logits_tile(ckv_ref[0, 0], 0)
    lax.fori_loop(0, nch_a, both, 0)
    lax.fori_loop(nch_a, nch_s, score_only, 0)

    @pl.when(t > 0)
    def _():
        value_tile(ckvt_ref[0, nch_a - 1], n_tiles - 1)
        o_lat_t = acct_sc[...] / l_sc[...]
        for h in range(N_HEADS):
            o_lat = o_lat_t[:, h * ROW_TILE:(h + 1) * ROW_TILE].T.astype(jnp.bfloat16)
            oh = jnp.dot(o_lat, wuv_ref[h], preferred_element_type=jnp.float32)
            o_ref[0, :, h * HEAD_DIM:(h + 1) * HEAD_DIM] = oh.astype(jnp.bfloat16)

    def threshold_search(n):
        def count_ge(cand):
            acc = jnp.zeros((8, ROW_TILE), jnp.int32)
            for c in range(n):
                hit = jnp.where(sc_sc[slot_s, c] >= cand, 1, 0)
                acc = acc + jnp.sum(hit.reshape(KEY_CHUNK // 8, 8, ROW_TILE), axis=0)
            return jnp.sum(acc, axis=0, keepdims=True)

        def count16_ge(cand):
            one = jnp.ones((), jnp.bfloat16)
            zero = jnp.zeros((), jnp.bfloat16)
            acc = jnp.zeros((16, ROW_TILE), jnp.float32)
            for c in range(n):
                hit = jnp.where(sc16_sc[slot_s, c] >= cand, one, zero).reshape(KEY_CHUNK // 16, 16, ROW_TILE)
                parts = [hit[i] for i in range(KEY_CHUNK // 16)]
                while len(parts) > 1:
                    parts = [parts[i] + parts[i + 1] for i in range(0, len(parts), 2)]
                acc = acc + parts[0].astype(jnp.float32)
            return jnp.sum(acc, axis=0, keepdims=True)

        def coarse_pass(i, base):
            cand = base + lax.shift_left(jnp.int32(1), 15 - i)
            raw = cand ^ (lax.shift_right_arithmetic(cand, 15) & 0x7FFF)
            cand_f = pltpu.bitcast(lax.shift_left(raw, 16), jnp.float32)
            cnt = count16_ge(cand_f.astype(jnp.bfloat16))
            return jnp.where(cnt >= k_top, cand, base)

        v16 = lax.fori_loop(0, 16, coarse_pass, jnp.full((1, ROW_TILE), -(2 ** 15), jnp.int32))
        few = v16 == -(2 ** 15)

        centre = lax.shift_left(v16, 16) + jnp.where(v16 < 0, 2 ** 16 - 1, 0)
        lo_k = centre - (2 ** 15 + 1)
        span = 3 * 2 ** 15 + 1

        def fine_pass(i, state):
            off, n_ge = state
            cand_off = off + lax.shift_left(jnp.int32(1), 16 - i)
            cnt = count_ge(_ordered_bits_to_f32(lo_k + cand_off))
            keep = jnp.logical_and(cnt >= k_top, cand_off < span)
            return jnp.where(keep, cand_off, off), jnp.where(keep, cnt, n_ge)

        off, n_ge = lax.fori_loop(0, 17, fine_pass, (jnp.zeros((1, ROW_TILE), jnp.int32),
                                                     jnp.zeros((1, ROW_TILE), jnp.int32)))
        thr_sc[...] = jnp.where(few, LOWEST, _ordered_bits_to_f32(lo_k + off))
        excess_sc[...] = jnp.where(few, 0, jnp.maximum(n_ge - k_top, 0))

    for n in range(1, n_chunks + 1):
        pl.when(jnp.logical_and(nch_s == n, t < n_blocks))(functools.partial(threshold_search, n))
    thr = thr_sc[...]
    excess = excess_sc[...]

    @pl.when(jnp.logical_and(jnp.max(excess) > 0, t < n_blocks))
    def _():
        surplus = excess.astype(jnp.float32)

        def demote(i, after):
            c = nch_s - 1 - i
            s = sc_sc[slot_s, c]
            tied = s == thr
            rank = after + jnp.dot(triu_ref[...], jnp.where(tied, 1.0, 0.0).astype(jnp.bfloat16),
                                   preferred_element_type=jnp.float32)
            sc_sc[slot_s, c] = jnp.where(jnp.logical_and(tied, rank <= surplus), -jnp.inf, s)
            return rank[0:1, :]

        lax.fori_loop(0, nch_s, demote, jnp.zeros((1, ROW_TILE), jnp.float32))


def _sparse_attention(qidx, widx, q, kidx_c, ckv_c, ckvt_c, wuk, wuv, k_top):
    B, tp, _ = q.shape
    nch = kidx_c.shape[1]
    nblk = tp // ROW_TILE

    def scored(bi, t):
        return (bi, jnp.minimum(t, nblk - 1), 0)

    def attended(bi, t):
        return (bi, jnp.maximum(t - 1, 0), 0)

    return pl.pallas_call(
        functools.partial(_attn_kernel, k_top, nblk),
        grid=(B, nblk + 1),
        in_specs=[
            pl.BlockSpec((1, ROW_TILE, P_QIDX), scored),
            pl.BlockSpec((1, ROW_TILE, P_WIDX), scored),
            pl.BlockSpec((1, ROW_TILE, P_Q), attended),
            pl.BlockSpec((1, nch, KEY_CHUNK, P_KIDX), lambda bi, t: (bi, 0, 0, 0)),
            pl.BlockSpec((1, nch, KEY_CHUNK, P_CKV), lambda bi, t: (bi, 0, 0, 0)),
            pl.BlockSpec((1, nch, P_CKV, KEY_CHUNK), lambda bi, t: (bi, 0, 0, 0)),
            _resident((N_HEADS, KV_RANK, HEAD_DIM)),
            _resident((N_HEADS, KV_RANK, HEAD_DIM)),
            _resident((KEY_CHUNK, KEY_CHUNK)),
        ],
        out_specs=pl.BlockSpec((1, ROW_TILE, P_Q), attended),
        out_shape=jax.ShapeDtypeStruct((B, tp, P_Q), jnp.bfloat16),
        scratch_shapes=[
            pltpu.VMEM((2, nch, KEY_CHUNK, ROW_TILE), jnp.float32),
            pltpu.VMEM((2, nch, KEY_CHUNK, ROW_TILE), jnp.bfloat16),
            pltpu.VMEM((IDX_DIM, N_COLS), jnp.bfloat16),
            pltpu.VMEM((KV_RANK, N_COLS), jnp.bfloat16),
            pltpu.VMEM((KV_RANK, N_COLS), jnp.float32),
            pltpu.VMEM((1, N_COLS), jnp.float32),
            pltpu.VMEM((1, N_COLS), jnp.float32),
            pltpu.VMEM((1, N_COLS), jnp.float32),
            pltpu.VMEM((KEY_CHUNK, N_COLS), jnp.bfloat16),
            pltpu.VMEM((KEY_CHUNK, COL_TILE), jnp.float32),
            pltpu.VMEM((1, ROW_TILE), jnp.float32),
            pltpu.VMEM((1, ROW_TILE), jnp.int32),
        ],
        compiler_params=_cparams(("parallel", "arbitrary")),
        name="sparse_attn",
    )(qidx, widx, q, kidx_c, ckv_c, ckvt_c, wuk, wuv, jnp.triu(jnp.ones((KEY_CHUNK, KEY_CHUNK), jnp.bfloat16)))


def _merge_kernel(alpha, x_ref, hg_ref, o_ref, ga_ref, gb_ref, wa_ref, wb_ref, wo_ref, g_ref, b_ref, y_ref):
    half = x_ref.shape[0] // 2
    for r in (slice(0, half), slice(half, 2 * half)):
        ya = jnp.dot(hg_ref[r, :], wa_ref[...], preferred_element_type=jnp.float32)
        yb = jnp.dot(o_ref[r, :], wb_ref[...], preferred_element_type=jnp.float32)
        mixed = ga_ref[r, :] * ya + gb_ref[r, :] * yb
        z = jnp.dot(mixed.astype(jnp.bfloat16), wo_ref[...], preferred_element_type=jnp.float32)
        y_ref[r, :] = _layer_norm(alpha * x_ref[r, :] + z, g_ref[...], b_ref[...])


def _merge(alpha, x2d, hg, o, ga, gb, w_a, w_b, w_o, g, b, tm):
    n = x2d.shape[0]

    def rows(w):
        return pl.BlockSpec((tm, w), lambda i: (i, 0))

    return pl.pallas_call(
        functools.partial(_merge_kernel, alpha),
        grid=(n // tm,),
        in_specs=[rows(D_MODEL), rows(D_RNN), rows(P_Q), rows(D_MODEL), rows(D_MODEL),
                  _resident((D_RNN, D_MODEL)), _resident((P_Q, D_MODEL)), _resident((D_MODEL, D_MODEL)),
                  _resident((1, D_MODEL)), _resident((1, D_MODEL))],
        out_specs=rows(D_MODEL),
        out_shape=jax.ShapeDtypeStruct((n, D_MODEL), jnp.float32),
        compiler_params=_cparams(("parallel",)),
        name="merge",
    )(x2d, hg, o, ga, gb, w_a, w_b, w_o, g, b)


def _mlp_kernel(alpha, x_ref, wu_ref, bu_ref, wd_ref, bd_ref, g_ref, b_ref, y_ref):
    x = x_ref[...]
    h = jnp.dot(x.astype(jnp.bfloat16), wu_ref[...], preferred_element_type=jnp.float32) + bu_ref[...]
    h = jnp.maximum(h, 0.0)
    h = (h * h).astype(jnp.bfloat16)
    z = jnp.dot(h, wd_ref[...], preferred_element_type=jnp.float32) + bd_ref[...]
    y_ref[...] = _layer_norm(alpha * x + z, g_ref[...], b_ref[...])


def _mlp(alpha, x2d, w_up, b_up, w_down, b_down, g, b, tm):
    n = x2d.shape[0]
    return pl.pallas_call(
        functools.partial(_mlp_kernel, alpha),
        grid=(n // tm,),
        in_specs=[pl.BlockSpec((tm, D_MODEL), lambda i: (i, 0)),
                  _resident((D_MODEL, D_FF)), _resident((1, D_FF)),
                  _resident((D_FF, D_MODEL)), _resident((1, D_MODEL)),
                  _resident((1, D_MODEL)), _resident((1, D_MODEL))],
        out_specs=pl.BlockSpec((tm, D_MODEL), lambda i: (i, 0)),
        out_shape=jax.ShapeDtypeStruct((n, D_MODEL), jnp.float32),
        compiler_params=_cparams(("parallel",)),
        name="mlp",
    )(x2d, w_up, b_up, w_down, b_down, g, b)


def _mlp_last(alpha, x2d, batch, rows, seq, w_up, b_up, w_down, b_down, g, b, tm):
    per_batch = seq // tm
    return pl.pallas_call(
        functools.partial(_mlp_kernel, alpha),
        grid=(batch, per_batch),
        in_specs=[pl.BlockSpec((pl.Element(tm), pl.Element(D_MODEL)),
                               lambda bi, i: (pl.multiple_of(bi * rows + ROW_TILE + i * tm, ROW_TILE), 0)),
                  _resident((D_MODEL, D_FF)), _resident((1, D_FF)),
                  _resident((D_FF, D_MODEL)), _resident((1, D_MODEL)),
                  _resident((1, D_MODEL)), _resident((1, D_MODEL))],
        out_specs=pl.BlockSpec((tm, D_MODEL), lambda bi, i: (bi * per_batch + i, 0)),
        out_shape=jax.ShapeDtypeStruct((batch * seq, D_MODEL), jnp.float32),
        compiler_params=_cparams(("parallel", "parallel")),
        name="mlp_last",
    )(x2d, w_up, b_up, w_down, b_down, g, b)


def _pad_in_proj(w_in):
    splits = (D_RNN, D_RNN, N_HEADS * HEAD_DIM, KV_RANK, IDX_HEADS * IDX_DIM, IDX_DIM, IDX_HEADS,
              D_MODEL, D_MODEL)
    padded = (P_LRU, P_G, P_Q, P_CKV, P_QIDX, P_KIDX, P_WIDX, P_GA, P_GB)
    parts, off = [], 0
    for w, pw in zip(splits, padded):
        part = w_in[:, off:off + w]
        if pw > w:
            part = jnp.pad(part, ((0, 0), (0, pw - w)))
        parts.append(part)
        off += w
    return jnp.concatenate(parts, axis=1).astype(jnp.bfloat16)


def kernel(x, meta_tokens, ln_in_g, ln_in_b, w_in, conv_w, conv_b, w_rg_a, b_rg_a, w_rg_x, b_rg_x, lru_lambda,
           kv_norm_g, w_uk, w_uv, w_branch_a, w_branch_b, w_out, ln1_g, ln1_b, w_up, b_up, w_down, b_down,
           ln2_g, ln2_b):
    B, S, _ = x.shape
    depth = w_in.shape[0]
    assert S % ROW_TILE == 0
    T = S + N_META
    k_top = min(TOPK_MAX, T // 4)
    alpha = (2.0 * depth) ** 0.25
    n_blk = S // ROW_TILE + 1
    tp = n_blk * ROW_TILE
    tk = -(-tp // KEY_CHUNK) * KEY_CHUNK
    n = B * tp
    tm_proj, tm = 320, 640
    assert tp % tm_proj == 0 and n % tm == 0
    bf16 = jnp.bfloat16

    def vec(a):
        return a.reshape(1, -1)

    meta_pad = jnp.pad(meta_tokens.astype(x.dtype), ((PADF, 0), (0, 0)))
    h = jnp.concatenate([jnp.broadcast_to(meta_pad[None], (B, ROW_TILE, D_MODEL)), x], axis=1).reshape(n, D_MODEL)

    for l in range(depth):
        rec = (conv_w[l], vec(conv_b[l]), w_rg_a[l].astype(bf16), vec(b_rg_a[l]), w_rg_x[l].astype(bf16),
               vec(b_rg_x[l]), vec(lru_lambda[l]))
        outs = _proj(h, _pad_in_proj(w_in[l]), vec(kv_norm_g[l]), vec(ln_in_g), vec(ln_in_b), rec, tm_proj,
                     tp // tm_proj, l == 0)
        if l == 0:
            h, *outs = outs
        hg, q, ckv, qidx, kidx, widx, ga, gb = outs

        key_pad = ((0, 0), (0, tk - tp), (0, 0))
        kidx_c = jnp.pad(kidx.reshape(B, tp, P_KIDX), key_pad).reshape(B, tk // KEY_CHUNK, KEY_CHUNK, P_KIDX)
        ckv_c = jnp.pad(ckv.reshape(B, tp, P_CKV), key_pad).reshape(B, tk // KEY_CHUNK, KEY_CHUNK, P_CKV)
        o = _sparse_attention(qidx.reshape(B, tp, P_QIDX), widx.reshape(B, tp, P_WIDX), q.reshape(B, tp, P_Q),
                              kidx_c, ckv_c, jnp.swapaxes(ckv_c, 2, 3),
                              w_uk[l].astype(bf16), w_uv[l].astype(bf16), k_top)

        h = _merge(alpha, h, hg, o.reshape(n, P_Q), ga, gb, w_branch_a[l].astype(bf16),
                   w_branch_b[l].astype(bf16), w_out[l].astype(bf16), vec(ln1_g[l]), vec(ln1_b[l]), tm)
        mlp_args = (w_up[l].astype(bf16), vec(b_up[l]), w_down[l].astype(bf16), vec(b_down[l]), vec(ln2_g[l]),
                    vec(ln2_b[l]))
        if l < depth - 1:
            h = _mlp(alpha, h, *mlp_args, tm)
    return _mlp_last(alpha, h, B, tp, S, *mlp_args, 512).reshape(B, S, D_MODEL)
```

```python
import functools
import math

import jax
import jax.numpy as jnp
from jax import lax
from jax.experimental import pallas as pl
from jax.experimental.pallas import tpu as pltpu

D_MODEL = 1024
N_META = 16
D_RNN = 1280
LRU_BLOCKS = 10
LRU_BLOCK = D_RNN // LRU_BLOCKS
CONV_WIDTH = 4
LRU_C = 8.0
N_HEADS = 8
HEAD_DIM = 128
KV_RANK = 256
IDX_HEADS = 8
IDX_DIM = 64
TOPK_MAX = 256
D_FF = 4 * D_MODEL
LN_EPS = 1e-5
NEG_INF = -1e30

LANES = 128
ROW_TILE = 128
PADF = ROW_TILE - N_META
KEY_CHUNK = 512
VMEM_LIMIT = 56 * 1024 * 1024

P_LRU, P_G, P_Q, P_CKV, P_QIDX, P_KIDX, P_WIDX, P_GA, P_GB = (
    D_RNN, D_RNN, N_HEADS * HEAD_DIM, KV_RANK, IDX_HEADS * IDX_DIM, LANES, LANES, D_MODEL, D_MODEL)
P_OFFS = []
_acc = 0
for _w in (P_LRU, P_G, P_Q, P_CKV, P_QIDX, P_KIDX, P_WIDX, P_GA, P_GB):
    P_OFFS.append(_acc)
    _acc += _w
P_TOTAL = _acc


def _cparams(sem):
    return pltpu.CompilerParams(dimension_semantics=sem, vmem_limit_bytes=VMEM_LIMIT)


def _resident(shape):
    nd = len(shape)
    return pl.BlockSpec(shape, lambda *_: (0,) * nd, pipeline_mode=pl.Buffered(1))


def _layer_norm(x, g, b):
    mu = jnp.mean(x, axis=-1, keepdims=True)
    d = x - mu
    var = jnp.mean(d * d, axis=-1, keepdims=True)
    return d * lax.rsqrt(var + LN_EPS) * g + b


def _sigmoid(x):
    return 0.5 * jnp.tanh(0.5 * x) + 0.5


def _recurrent_cols(x, gelu, cs, first, cw_ref, cb_ref, wa_ref, ba_ref, wx_ref, bx_ref, lam_ref, xe_sc, h_sc):
    tm, width = x.shape
    row = lax.broadcasted_iota(jnp.int32, (tm, 1), 0)
    real = jnp.logical_or(jnp.logical_not(first), row >= PADF)
    x = jnp.where(real, x, 0.0)
    xe_sc[8:8 + tm, cs] = x
    cw = cw_ref[:, cs]
    xc = (cw[3:4] * x + cw[2:3] * xe_sc[7:7 + tm, cs] + cw[1:2] * xe_sc[6:6 + tm, cs]
          + cw[0:1] * xe_sc[5:5 + tm, cs] + cb_ref[:, cs])
    xe_sc[0:8, cs] = xe_sc[tm:tm + 8, cs]

    xcb = xc.astype(jnp.bfloat16)
    z = -lam_ref[:, cs]
    softplus = jnp.maximum(z, 0.0) + jnp.log1p(jnp.exp(-jnp.abs(z)))
    a_parts, u_parts = [], []
    for j in range(width // LRU_BLOCK):
        n = cs.start // LRU_BLOCK + j
        sl = slice(j * LRU_BLOCK, (j + 1) * LRU_BLOCK)
        gl = slice(cs.start + j * LRU_BLOCK, cs.start + (j + 1) * LRU_BLOCK)
        xn = xcb[:, sl]
        r = _sigmoid(jnp.dot(xn, wa_ref[n], preferred_element_type=jnp.float32) + ba_ref[:, gl])
        i = _sigmoid(jnp.dot(xn, wx_ref[n], preferred_element_type=jnp.float32) + bx_ref[:, gl])
        log_a = -LRU_C * r * softplus[:, sl]
        a = jnp.exp(log_a)
        u = jnp.sqrt(-jnp.tanh(log_a) * (a * a + 1.0)) * (i * xc[:, sl])
        a_parts.append(a)
        u_parts.append(jnp.where(real, u, 0.0))
    a = jnp.concatenate(a_parts, axis=1)
    u = jnp.concatenate(u_parts, axis=1)

    groups = tm // 8
    a = a.reshape(groups, 8, width)
    u = u.reshape(groups, 8, width)
    sub = lax.broadcasted_iota(jnp.int32, (1, 8, 1), 1)
    for d in (1, 2, 4):
        later = sub >= d
        u = jnp.where(later, a * pltpu.roll(u, d, axis=1) + u, u)
        a = jnp.where(later, a * pltpu.roll(a, d, axis=1), a)
    h = h_sc[:, cs]
    out = []
    for gi in range(groups):
        hg = a[gi] * h + u[gi]
        out.append(hg)
        h = hg[7:8, :]
    h_sc[:, cs] = h
    return jnp.concatenate(out, axis=0) * gelu


REC_COLS = 2 * LRU_BLOCK


def _proj_kernel(normalize, tiles_per_batch, *refs):
    first = lax.rem(pl.program_id(0), tiles_per_batch) == 0
    if normalize:
        head_ref, x_ref, *refs = refs
    else:
        x_ref, *refs = refs
    w_ref, kvg_ref, lng_ref, lnb_ref, cw_ref, cb_ref, wa_ref, ba_ref, wx_ref, bx_ref, lam_ref, *refs = refs
    if normalize:
        h_ref, *refs = refs
        h = _layer_norm(jnp.where(first, head_ref[...], x_ref[...]), lng_ref[...], lnb_ref[...])
        h_ref[...] = h
        xb = h.astype(jnp.bfloat16)
    else:
        xb = x_ref[...].astype(jnp.bfloat16)
    hg_ref, q_ref, ckv_ref, qidx_ref, kidx_ref, widx_ref, ga_ref, gb_ref, xe_sc, h_sc = refs

    def cols(i, lo, width):
        return jnp.dot(xb, w_ref[:, P_OFFS[i] + lo:P_OFFS[i] + lo + width], preferred_element_type=jnp.float32)

    @pl.when(first)
    def _():
        xe_sc[0:8, :] = jnp.zeros((8, D_RNN), jnp.float32)
        h_sc[...] = jnp.zeros_like(h_sc)

    def normed_latents():
        c = cols(3, 0, P_CKV)
        ms = jnp.mean(c * c, axis=-1, keepdims=True)
        ckv_ref[...] = (c * lax.rsqrt(ms + LN_EPS) * kvg_ref[...]).astype(jnp.bfloat16)

    def index_keys():
        k = cols(5, 0, P_KIDX)
        lane = lax.broadcasted_iota(jnp.int32, k.shape, 1)
        mu = jnp.sum(k, axis=-1, keepdims=True) * (1.0 / IDX_DIM)
        d = jnp.where(lane < IDX_DIM, k - mu, 0.0)
        var = jnp.sum(d * d, axis=-1, keepdims=True) * (1.0 / IDX_DIM)
        kidx_ref[...] = (d * lax.rsqrt(var + LN_EPS)).astype(jnp.bfloat16)
        widx_ref[...] = cols(6, 0, P_WIDX) * (IDX_HEADS ** -0.5 * IDX_DIM ** -0.5)

    def plain(ref, i, lo, width, act=None):
        def run():
            y = cols(i, lo, width)
            ref[:, lo:lo + width] = (y if act is None else act(y)).astype(ref.dtype)
        return run

    half = D_MODEL // 2
    others = [plain(q_ref, 2, 0, half), plain(q_ref, 2, half, half), normed_latents, plain(qidx_ref, 4, 0, P_QIDX),
              index_keys, plain(ga_ref, 7, 0, half, _sigmoid), plain(ga_ref, 7, half, half, _sigmoid),
              plain(gb_ref, 8, 0, half, _sigmoid), plain(gb_ref, 8, half, half, _sigmoid)]
    n_slices = D_RNN // REC_COLS
    per_slice = -(-len(others) // n_slices)
    for s in range(n_slices):
        cs = slice(s * REC_COLS, (s + 1) * REC_COLS)
        g = cols(1, cs.start, REC_COLS)
        gelu = 0.5 * g * (1.0 + jnp.tanh(math.sqrt(2.0 / math.pi) * (g + 0.044715 * (g * g * g))))
        hg_ref[:, cs] = _recurrent_cols(cols(0, cs.start, REC_COLS), gelu, cs, first, cw_ref, cb_ref, wa_ref, ba_ref,
                                        wx_ref, bx_ref, lam_ref, xe_sc, h_sc).astype(jnp.bfloat16)
        for run in others[s * per_slice:(s + 1) * per_slice]:
            run()


def _proj(acts, w_in_p, kv_g, ln_g, ln_b, rec, tm, tiles_per_batch, n):
    normalize = len(acts) == 2
    f32, bf16 = jnp.float32, jnp.bfloat16
    widths = (D_RNN, P_Q, P_CKV, P_QIDX, P_KIDX, P_WIDX, P_GA, P_GB)
    dtypes = (bf16, bf16, bf16, bf16, bf16, f32, f32, f32)
    if normalize:
        widths, dtypes = (D_MODEL,) + widths, (f32,) + dtypes
        seq = acts[1].shape[0] // (n // (tiles_per_batch * tm))
        lead = tiles_per_batch * tm - seq

        def seq_offset(i):
            bi, j = i // tiles_per_batch, lax.rem(i, tiles_per_batch)
            return pl.multiple_of(bi * seq + jnp.maximum(j * tm - lead, 0), 8), 0

        act_specs = [pl.BlockSpec((tm, D_MODEL), lambda i: (i // tiles_per_batch, 0)),
                     pl.BlockSpec((pl.Element(tm), pl.Element(D_MODEL)), seq_offset)]
    else:
        act_specs = [pl.BlockSpec((tm, D_MODEL), lambda i: (i, 0))]
    return pl.pallas_call(
        functools.partial(_proj_kernel, normalize, tiles_per_batch),
        grid=(n // tm,),
        in_specs=act_specs + [
            _resident((D_MODEL, P_TOTAL)),
            _resident((1, KV_RANK)),
            _resident((1, D_MODEL)),
            _resident((1, D_MODEL)),
            _resident((CONV_WIDTH, D_RNN)),
            _resident((1, D_RNN)),
            _resident((LRU_BLOCKS, LRU_BLOCK, LRU_BLOCK)),
            _resident((1, D_RNN)),
            _resident((LRU_BLOCKS, LRU_BLOCK, LRU_BLOCK)),
            _resident((1, D_RNN)),
            _resident((1, D_RNN)),
        ],
        out_specs=[pl.BlockSpec((tm, w), lambda i: (i, 0)) for w in widths],
        out_shape=[jax.ShapeDtypeStruct((n, w), dt) for w, dt in zip(widths, dtypes)],
        scratch_shapes=[
            pltpu.VMEM((tm + 8, D_RNN), jnp.float32),
            pltpu.VMEM((1, D_RNN), jnp.float32),
        ],
        compiler_params=_cparams(("arbitrary",)),
        name="proj",
    )(*acts, w_in_p, kv_g, ln_g, ln_b, *rec)


N_COLS = N_HEADS * ROW_TILE
COL_TILE = 2 * ROW_TILE
LOWEST = -3.0e38


def _ordered_bits_to_f32(k):
    return pltpu.bitcast(k ^ (lax.shift_right_arithmetic(k, 31) & 0x7FFFFFFF), jnp.float32)


def _attn_kernel(k_top, n_blocks, qi_ref, wi_ref, q_ref, kidx_ref, ckv_ref, ckvt_ref, wuk_ref, wuv_ref, triu_ref,
                 o_ref, sc_sc, sc16_sc, qit_sc, qabst_sc, acct_sc, m_sc, l_sc, alpha_sc, p_sc, s0_sc, thr_sc,
                 excess_sc):
    t = pl.program_id(1)
    n_chunks = sc_sc.shape[1]
    chunk_blocks = KEY_CHUNK // ROW_TILE
    blk_s = jnp.minimum(t, n_blocks - 1)
    nch_s = blk_s // chunk_blocks + 1
    nch_a = jnp.where(t > 0, (t - 1) // chunk_blocks + 1, 0)
    slot_s = lax.rem(t, 2)
    slot_a = 1 - slot_s
    q_pos = blk_s * ROW_TILE + lax.broadcasted_iota(jnp.int32, (1, ROW_TILE), 1)
    k_iota = lax.broadcasted_iota(jnp.int32, (KEY_CHUNK, 1), 0)

    qit = qi_ref[0].astype(jnp.float32).T
    for h in range(IDX_HEADS):
        qit_sc[:, h * ROW_TILE:(h + 1) * ROW_TILE] = qit[h * IDX_DIM:(h + 1) * IDX_DIM, :].astype(jnp.bfloat16)
    wit = wi_ref[0].T
    wi_row = jnp.concatenate([wit[h:h + 1, :] for h in range(IDX_HEADS)], axis=1)
    qt = q_ref[0].astype(jnp.float32).T
    scale = HEAD_DIM ** -0.5 * math.log2(math.e)
    for h in range(N_HEADS):
        qa = jnp.dot(wuk_ref[h], qt[h * HEAD_DIM:(h + 1) * HEAD_DIM, :].astype(jnp.bfloat16),
                     preferred_element_type=jnp.float32)
        qabst_sc[:, h * ROW_TILE:(h + 1) * ROW_TILE] = (qa * scale).astype(jnp.bfloat16)

    def score_chunk(c):
        ks = kidx_ref[0, c][:, 0:IDX_DIM]
        sc = None
        for j in range(N_COLS // COL_TILE):
            cols = slice(j * COL_TILE, (j + 1) * COL_TILE)
            lg = jnp.dot(ks, qit_sc[:, cols], preferred_element_type=jnp.float32)
            w = jnp.maximum(lg, 0.0) * wi_row[:, cols]
            part = w[:, :ROW_TILE] + w[:, ROW_TILE:]
            sc = part if sc is None else sc + part
        k_pos = c * KEY_CHUNK + k_iota
        visible = jnp.logical_and(k_pos <= q_pos, k_pos >= PADF)
        sc = jnp.where(visible, sc, -jnp.inf)
        sc_sc[slot_s, c] = sc
        sc16_sc[slot_s, c] = sc.astype(jnp.bfloat16)

    @pl.when(t == 0)
    def _():
        thr_sc[...] = jnp.full_like(thr_sc, LOWEST)

    thr_a = thr_sc[...]
    n_tiles = N_COLS // COL_TILE
    last = slice((n_tiles - 1) * COL_TILE, n_tiles * COL_TILE)
    m_sc[...] = jnp.full_like(m_sc, NEG_INF)
    l_sc[...] = jnp.zeros_like(l_sc)
    acct_sc[...] = jnp.zeros_like(acct_sc)
    alpha_sc[:, last] = jnp.ones((1, COL_TILE), jnp.float32)
    p_sc[:, last] = jnp.zeros((KEY_CHUNK, COL_TILE), jnp.bfloat16)

    def logits_tile(kc, j):
        cols = slice(j * COL_TILE, (j + 1) * COL_TILE)
        return jnp.dot(kc, qabst_sc[:, cols], preferred_element_type=jnp.float32)

    def softmax_tile(s, sel, j):
        cols = slice(j * COL_TILE, (j + 1) * COL_TILE)
        s = jnp.concatenate([jnp.where(sel, s[:, :ROW_TILE], NEG_INF),
                             jnp.where(sel, s[:, ROW_TILE:], NEG_INF)], axis=1)
        m_prev = m_sc[:, cols]
        m_new = jnp.maximum(m_prev, jnp.max(s, axis=0, keepdims=True))
        p = jnp.exp2(s - m_new)
        alpha = jnp.exp2(m_prev - m_new)
        l_sc[:, cols] = alpha * l_sc[:, cols] + jnp.sum(p, axis=0, keepdims=True)
        m_sc[:, cols] = m_new
        alpha_sc[:, cols] = alpha
        p_sc[:, cols] = p.astype(jnp.bfloat16)

    def value_tile(kct, j):
        cols = slice(j * COL_TILE, (j + 1) * COL_TILE)
        acct_sc[:, cols] = alpha_sc[:, cols] * acct_sc[:, cols] + jnp.dot(
            kct, p_sc[:, cols], preferred_element_type=jnp.float32)

    def attn_chunk(c):
        kc = ckv_ref[0, c]
        kct = ckvt_ref[0, c]
        sel = sc_sc[slot_a, c] >= thr_a
        softmax_tile(s0_sc[...], sel, 0)
        value_tile(ckvt_ref[0, jnp.maximum(c - 1, 0)], n_tiles - 1)
        for j in range(1, n_tiles):
            softmax_tile(logits_tile(kc, j), sel, j)
            value_tile(kct, j - 1)
        s0_sc[...] = logits_tile(ckv_ref[0, jnp.minimum(c + 1, n_chunks - 1)], 0)

    def both(c, carry):
        attn_chunk(c)
        score_chunk(c)
        return carry

    def score_only(c, carry):
        score_chunk(c)
        return carry

    s0_sc[...] = logits_tile(ckv_ref[0, 0], 0)
    lax.fori_loop(0, nch_a, both, 0)
    lax.fori_loop(nch_a, nch_s, score_only, 0)

    @pl.when(t > 0)
    def _():
        value_tile(ckvt_ref[0, nch_a - 1], n_tiles - 1)
        o_lat_t = acct_sc[...] / l_sc[...]
        for h in range(N_HEADS):
            o_lat = o_lat_t[:, h * ROW_TILE:(h + 1) * ROW_TILE].T.astype(jnp.bfloat16)
            oh = jnp.dot(o_lat, wuv_ref[h], preferred_element_type=jnp.float32)
            o_ref[0, :, h * HEAD_DIM:(h + 1) * HEAD_DIM] = oh.astype(jnp.bfloat16)

    def threshold_search(n):
        def count_ge(cand):
            acc = jnp.zeros((8, ROW_TILE), jnp.int32)
            for c in range(n):
                hit = jnp.where(sc_sc[slot_s, c] >= cand, 1, 0)
                acc = acc + jnp.sum(hit.reshape(KEY_CHUNK // 8, 8, ROW_TILE), axis=0)
            return jnp.sum(acc, axis=0, keepdims=True)

        def count16_ge(cand):
            one = jnp.ones((), jnp.bfloat16)
            zero = jnp.zeros((), jnp.bfloat16)
            acc = jnp.zeros((16, ROW_TILE), jnp.float32)
            for c in range(n):
                hit = jnp.where(sc16_sc[slot_s, c] >= cand, one, zero).reshape(KEY_CHUNK // 16, 16, ROW_TILE)
                parts = [hit[i] for i in range(KEY_CHUNK // 16)]
                while len(parts) > 1:
                    parts = [parts[i] + parts[i + 1] for i in range(0, len(parts), 2)]
                acc = acc + parts[0].astype(jnp.float32)
            return jnp.sum(acc, axis=0, keepdims=True)

        def coarse_pass(i, base):
            cand = base + lax.shift_left(jnp.int32(1), 15 - i)
            raw = cand ^ (lax.shift_right_arithmetic(cand, 15) & 0x7FFF)
            cand_f = pltpu.bitcast(lax.shift_left(raw, 16), jnp.float32)
            cnt = count16_ge(cand_f.astype(jnp.bfloat16))
            return jnp.where(cnt >= k_top, cand, base)

        v16 = lax.fori_loop(0, 16, coarse_pass, jnp.full((1, ROW_TILE), -(2 ** 15), jnp.int32))
        few = v16 == -(2 ** 15)

        centre = lax.shift_left(v16, 16) + jnp.where(v16 < 0, 2 ** 16 - 1, 0)
        lo_k = centre - (2 ** 15 + 1)
        span = 3 * 2 ** 15 + 1

        def fine_pass(i, state):
            off, n_ge = state
            cand_off = off + lax.shift_left(jnp.int32(1), 16 - i)
            cnt = count_ge(_ordered_bits_to_f32(lo_k + cand_off))
            keep = jnp.logical_and(cnt >= k_top, cand_off < span)
            return jnp.where(keep, cand_off, off), jnp.where(keep, cnt, n_ge)

        off, n_ge = lax.fori_loop(0, 17, fine_pass, (jnp.zeros((1, ROW_TILE), jnp.int32),
                                                     jnp.zeros((1, ROW_TILE), jnp.int32)))
        thr_sc[...] = jnp.where(few, LOWEST, _ordered_bits_to_f32(lo_k + off))
        excess_sc[...] = jnp.where(few, 0, jnp.maximum(n_ge - k_top, 0))

    for n in range(1, n_chunks + 1):
        pl.when(jnp.logical_and(nch_s == n, t < n_blocks))(functools.partial(threshold_search, n))
    thr = thr_sc[...]
    excess = excess_sc[...]

    @pl.when(jnp.logical_and(jnp.max(excess) > 0, t < n_blocks))
    def _():
        surplus = excess.astype(jnp.float32)

        def demote(i, after):
            c = nch_s - 1 - i
            s = sc_sc[slot_s, c]
            tied = s == thr
            rank = after + jnp.dot(triu_ref[...], jnp.where(tied, 1.0, 0.0).astype(jnp.bfloat16),
                                   preferred_element_type=jnp.float32)
            sc_sc[slot_s, c] = jnp.where(jnp.logical_and(tied, rank <= surplus), -jnp.inf, s)
            return rank[0:1, :]

        lax.fori_loop(0, nch_s, demote, jnp.zeros((1, ROW_TILE), jnp.float32))


def _sparse_attention(qidx, widx, q, kidx_c, ckv_c, ckvt_c, wuk, wuv, k_top):
    B, tp, _ = q.shape
    nch = kidx_c.shape[1]
    nblk = tp // ROW_TILE

    def scored(bi, t):
        return (bi, jnp.minimum(t, nblk - 1), 0)

    def attended(bi, t):
        return (bi, jnp.maximum(t - 1, 0), 0)

    return pl.pallas_call(
        functools.partial(_attn_kernel, k_top, nblk),
        grid=(B, nblk + 1),
        in_specs=[
            pl.BlockSpec((1, ROW_TILE, P_QIDX), scored),
            pl.BlockSpec((1, ROW_TILE, P_WIDX), scored),
            pl.BlockSpec((1, ROW_TILE, P_Q), attended),
            pl.BlockSpec((1, nch, KEY_CHUNK, P_KIDX), lambda bi, t: (bi, 0, 0, 0)),
            pl.BlockSpec((1, nch, KEY_CHUNK, P_CKV), lambda bi, t: (bi, 0, 0, 0)),
            pl.BlockSpec((1, nch, P_CKV, KEY_CHUNK), lambda bi, t: (bi, 0, 0, 0)),
            _resident((N_HEADS, KV_RANK, HEAD_DIM)),
            _resident((N_HEADS, KV_RANK, HEAD_DIM)),
            _resident((KEY_CHUNK, KEY_CHUNK)),
        ],
        out_specs=pl.BlockSpec((1, ROW_TILE, P_Q), attended),
        out_shape=jax.ShapeDtypeStruct((B, tp, P_Q), jnp.bfloat16),
        scratch_shapes=[
            pltpu.VMEM((2, nch, KEY_CHUNK, ROW_TILE), jnp.float32),
            pltpu.VMEM((2, nch, KEY_CHUNK, ROW_TILE), jnp.bfloat16),
            pltpu.VMEM((IDX_DIM, N_COLS), jnp.bfloat16),
            pltpu.VMEM((KV_RANK, N_COLS), jnp.bfloat16),
            pltpu.VMEM((KV_RANK, N_COLS), jnp.float32),
            pltpu.VMEM((1, N_COLS), jnp.float32),
            pltpu.VMEM((1, N_COLS), jnp.float32),
            pltpu.VMEM((1, N_COLS), jnp.float32),
            pltpu.VMEM((KEY_CHUNK, N_COLS), jnp.bfloat16),
            pltpu.VMEM((KEY_CHUNK, COL_TILE), jnp.float32),
            pltpu.VMEM((1, ROW_TILE), jnp.float32),
            pltpu.VMEM((1, ROW_TILE), jnp.int32),
        ],
        compiler_params=_cparams(("parallel", "arbitrary")),
        name="sparse_attn",
    )(qidx, widx, q, kidx_c, ckv_c, ckvt_c, wuk, wuv, jnp.triu(jnp.ones((KEY_CHUNK, KEY_CHUNK), jnp.bfloat16)))


def _merge_kernel(alpha, x_ref, hg_ref, o_ref, ga_ref, gb_ref, wa_ref, wb_ref, wo_ref, g_ref, b_ref, y_ref):
    half = x_ref.shape[0] // 2
    for r in (slice(0, half), slice(half, 2 * half)):
        ya = jnp.dot(hg_ref[r, :], wa_ref[...], preferred_element_type=jnp.float32)
        yb = jnp.dot(o_ref[r, :], wb_ref[...], preferred_element_type=jnp.float32)
        mixed = ga_ref[r, :] * ya + gb_ref[r, :] * yb
        z = jnp.dot(mixed.astype(jnp.bfloat16), wo_ref[...], preferred_element_type=jnp.float32)
        y_ref[r, :] = _layer_norm(alpha * x_ref[r, :] + z, g_ref[...], b_ref[...])


def _merge(alpha, x2d, hg, o, ga, gb, w_a, w_b, w_o, g, b, tm):
    n = x2d.shape[0]

    def rows(w):
        return pl.BlockSpec((tm, w), lambda i: (i, 0))

    return pl.pallas_call(
        functools.partial(_merge_kernel, alpha),
        grid=(n // tm,),
        in_specs=[rows(D_MODEL), rows(D_RNN), rows(P_Q), rows(D_MODEL), rows(D_MODEL),
                  _resident((D_RNN, D_MODEL)), _resident((P_Q, D_MODEL)), _resident((D_MODEL, D_MODEL)),
                  _resident((1, D_MODEL)), _resident((1, D_MODEL))],
        out_specs=rows(D_MODEL),
        out_shape=jax.ShapeDtypeStruct((n, D_MODEL), jnp.float32),
        compiler_params=_cparams(("parallel",)),
        name="merge",
    )(x2d, hg, o, ga, gb, w_a, w_b, w_o, g, b)


def _mlp_kernel(alpha, x_ref, wu_ref, bu_ref, wd_ref, bd_ref, g_ref, b_ref, y_ref):
    x = x_ref[...]
    h = jnp.dot(x.astype(jnp.bfloat16), wu_ref[...], preferred_element_type=jnp.float32) + bu_ref[...]
    h = jnp.maximum(h, 0.0)
    h = (h * h).astype(jnp.bfloat16)
    z = jnp.dot(h, wd_ref[...], preferred_element_type=jnp.float32) + bd_ref[...]
    y_ref[...] = _layer_norm(alpha * x + z, g_ref[...], b_ref[...])


def _mlp(alpha, x2d, w_up, b_up, w_down, b_down, g, b, tm):
    n = x2d.shape[0]
    return pl.pallas_call(
        functools.partial(_mlp_kernel, alpha),
        grid=(n // tm,),
        in_specs=[pl.BlockSpec((tm, D_MODEL), lambda i: (i, 0)),
                  _resident((D_MODEL, D_FF)), _resident((1, D_FF)),
                  _resident((D_FF, D_MODEL)), _resident((1, D_MODEL)),
                  _resident((1, D_MODEL)), _resident((1, D_MODEL))],
        out_specs=pl.BlockSpec((tm, D_MODEL), lambda i: (i, 0)),
        out_shape=jax.ShapeDtypeStruct((n, D_MODEL), jnp.float32),
        compiler_params=_cparams(("parallel",)),
        name="mlp",
    )(x2d, w_up, b_up, w_down, b_down, g, b)


def _mlp_last(alpha, x2d, batch, rows, seq, w_up, b_up, w_down, b_down, g, b, tm):
    per_batch = seq // tm
    return pl.pallas_call(
        functools.partial(_mlp_kernel, alpha),
        grid=(batch, per_batch),
        in_specs=[pl.BlockSpec((pl.Element(tm), pl.Element(D_MODEL)),
                               lambda bi, i: (pl.multiple_of(bi * rows + ROW_TILE + i * tm, ROW_TILE), 0)),
                  _resident((D_MODEL, D_FF)), _resident((1, D_FF)),
                  _resident((D_FF, D_MODEL)), _resident((1, D_MODEL)),
                  _resident((1, D_MODEL)), _resident((1, D_MODEL))],
        out_specs=pl.BlockSpec((tm, D_MODEL), lambda bi, i: (bi * per_batch + i, 0)),
        out_shape=jax.ShapeDtypeStruct((batch * seq, D_MODEL), jnp.float32),
        compiler_params=_cparams(("parallel", "parallel")),
        name="mlp_last",
    )(x2d, w_up, b_up, w_down, b_down, g, b)


def _pad_in_proj(w_in):
    splits = (D_RNN, D_RNN, N_HEADS * HEAD_DIM, KV_RANK, IDX_HEADS * IDX_DIM, IDX_DIM, IDX_HEADS,
              D_MODEL, D_MODEL)
    padded = (P_LRU, P_G, P_Q, P_CKV, P_QIDX, P_KIDX, P_WIDX, P_GA, P_GB)
    parts, off = [], 0
    for w, pw in zip(splits, padded):
        part = w_in[:, off:off + w]
        if pw > w:
            part = jnp.pad(part, ((0, 0), (0, pw - w)))
        parts.append(part)
        off += w
    return jnp.concatenate(parts, axis=1).astype(jnp.bfloat16)


def kernel(x, meta_tokens, ln_in_g, ln_in_b, w_in, conv_w, conv_b, w_rg_a, b_rg_a, w_rg_x, b_rg_x, lru_lambda,
           kv_norm_g, w_uk, w_uv, w_branch_a, w_branch_b, w_out, ln1_g, ln1_b, w_up, b_up, w_down, b_down,
           ln2_g, ln2_b):
    B, S, _ = x.shape
    depth = w_in.shape[0]
    assert S % ROW_TILE == 0
    T = S + N_META
    k_top = min(TOPK_MAX, T // 4)
    alpha = (2.0 * depth) ** 0.25
    n_blk = S // ROW_TILE + 1
    tp = n_blk * ROW_TILE
    tk = -(-tp // KEY_CHUNK) * KEY_CHUNK
    n = B * tp
    tm_proj, tm = 320, 640
    assert tp % tm_proj == 0 and n % tm == 0
    bf16 = jnp.bfloat16

    def vec(a):
        return a.reshape(1, -1)

    meta_pad = jnp.pad(meta_tokens.astype(x.dtype), ((PADF, 0), (0, 0)))
    head = jnp.concatenate([jnp.broadcast_to(meta_pad[None], (B, ROW_TILE, D_MODEL)),
                            x[:, :tm_proj - ROW_TILE]], axis=1).reshape(B * tm_proj, D_MODEL)
    acts = (head, x.reshape(B * S, D_MODEL))

    for l in range(depth):
        rec = (conv_w[l], vec(conv_b[l]), w_rg_a[l].astype(bf16), vec(b_rg_a[l]), w_rg_x[l].astype(bf16),
               vec(b_rg_x[l]), vec(lru_lambda[l]))
        outs = _proj(acts, _pad_in_proj(w_in[l]), vec(kv_norm_g[l]), vec(ln_in_g), vec(ln_in_b), rec, tm_proj,
                     tp // tm_proj, n)
        if l == 0:
            h, *outs = outs
        hg, q, ckv, qidx, kidx, widx, ga, gb = outs

        key_pad = ((0, 0), (0, tk - tp), (0, 0))
        kidx_c = jnp.pad(kidx.reshape(B, tp, P_KIDX), key_pad).reshape(B, tk // KEY_CHUNK, KEY_CHUNK, P_KIDX)
        ckv_c = jnp.pad(ckv.reshape(B, tp, P_CKV), key_pad).reshape(B, tk // KEY_CHUNK, KEY_CHUNK, P_CKV)
        o = _sparse_attention(qidx.reshape(B, tp, P_QIDX), widx.reshape(B, tp, P_WIDX), q.reshape(B, tp, P_Q),
                              kidx_c, ckv_c, jnp.swapaxes(ckv_c, 2, 3),
                              w_uk[l].astype(bf16), w_uv[l].astype(bf16), k_top)

        h = _merge(alpha, h, hg, o.reshape(n, P_Q), ga, gb, w_branch_a[l].astype(bf16),
                   w_branch_b[l].astype(bf16), w_out[l].astype(bf16), vec(ln1_g[l]), vec(ln1_b[l]), tm)
        mlp_args = (w_up[l].astype(bf16), vec(b_up[l]), w_down[l].astype(bf16), vec(b_down[l]), vec(ln2_g[l]),
                    vec(ln2_b[l]))
        if l < depth - 1:
            h = _mlp(alpha, h, *mlp_args, tm)
            acts = (h,)
    return _mlp_last(alpha, h, B, tp, S, *mlp_args, 512).reshape(B, S, D_MODEL)
```

```python
import functools
import math

import jax
import jax.numpy as jnp
from jax import lax
from jax.experimental import pallas as pl
from jax.experimental.pallas import tpu as pltpu

D_MODEL = 1024
N_META = 16
D_RNN = 1280
LRU_BLOCKS = 10
LRU_BLOCK = D_RNN // LRU_BLOCKS
CONV_WIDTH = 4
LRU_C = 8.0
N_HEADS = 8
HEAD_DIM = 128
KV_RANK = 256
IDX_HEADS = 8
IDX_DIM = 64
TOPK_MAX = 256
D_FF = 4 * D_MODEL
LN_EPS = 1e-5
NEG_INF = -1e30

LANES = 128
ROW_TILE = 128
PADF = ROW_TILE - N_META
KEY_CHUNK = 512
VMEM_LIMIT = 56 * 1024 * 1024

P_LRU, P_G, P_Q, P_CKV, P_QIDX, P_KIDX, P_WIDX, P_GA, P_GB = (
    D_RNN, D_RNN, N_HEADS * HEAD_DIM, KV_RANK, IDX_HEADS * IDX_DIM, LANES, LANES, D_MODEL, D_MODEL)
P_OFFS = []
_acc = 0
for _w in (P_LRU, P_G, P_Q, P_CKV, P_QIDX, P_KIDX, P_WIDX, P_GA, P_GB):
    P_OFFS.append(_acc)
    _acc += _w
P_TOTAL = _acc


def _cparams(sem):
    return pltpu.CompilerParams(dimension_semantics=sem, vmem_limit_bytes=VMEM_LIMIT)


def _resident(shape):
    nd = len(shape)
    return pl.BlockSpec(shape, lambda *_: (0,) * nd, pipeline_mode=pl.Buffered(1))


def _layer_norm(x, g, b):
    mu = jnp.mean(x, axis=-1, keepdims=True)
    d = x - mu
    var = jnp.mean(d * d, axis=-1, keepdims=True)
    return d * lax.rsqrt(var + LN_EPS) * g + b


def _sigmoid(x):
    return 0.5 * jnp.tanh(0.5 * x) + 0.5


def _recurrent_cols(x, gelu, cs, first, cw_ref, cb_ref, wa_ref, ba_ref, wx_ref, bx_ref, lam_ref, xe_sc, h_sc):
    tm, width = x.shape
    row = lax.broadcasted_iota(jnp.int32, (tm, 1), 0)
    real = jnp.logical_or(jnp.logical_not(first), row >= PADF)
    x = jnp.where(real, x, 0.0)
    xe_sc[8:8 + tm, cs] = x
    cw = cw_ref[:, cs]
    xc = (cw[3:4] * x + cw[2:3] * xe_sc[7:7 + tm, cs] + cw[1:2] * xe_sc[6:6 + tm, cs]
          + cw[0:1] * xe_sc[5:5 + tm, cs] + cb_ref[:, cs])
    xe_sc[0:8, cs] = xe_sc[tm:tm + 8, cs]

    xcb = xc.astype(jnp.bfloat16)
    z = -lam_ref[:, cs]
    softplus = jnp.maximum(z, 0.0) + jnp.log1p(jnp.exp(-jnp.abs(z)))
    a_parts, u_parts = [], []
    for j in range(width // LRU_BLOCK):
        n = cs.start // LRU_BLOCK + j
        sl = slice(j * LRU_BLOCK, (j + 1) * LRU_BLOCK)
        gl = slice(cs.start + j * LRU_BLOCK, cs.start + (j + 1) * LRU_BLOCK)
        xn = xcb[:, sl]
        r = _sigmoid(jnp.dot(xn, wa_ref[n], preferred_element_type=jnp.float32) + ba_ref[:, gl])
        i = _sigmoid(jnp.dot(xn, wx_ref[n], preferred_element_type=jnp.float32) + bx_ref[:, gl])
        log_a = -LRU_C * r * softplus[:, sl]
        a = jnp.exp(log_a)
        u = jnp.sqrt(-jnp.tanh(log_a) * (a * a + 1.0)) * (i * xc[:, sl])
        a_parts.append(a)
        u_parts.append(jnp.where(real, u, 0.0))
    a = jnp.concatenate(a_parts, axis=1)
    u = jnp.concatenate(u_parts, axis=1)

    groups = tm // 8
    a = a.reshape(groups, 8, width)
    u = u.reshape(groups, 8, width)
    sub = lax.broadcasted_iota(jnp.int32, (1, 8, 1), 1)
    for d in (1, 2, 4):
        later = sub >= d
        u = jnp.where(later, a * pltpu.roll(u, d, axis=1) + u, u)
        a = jnp.where(later, a * pltpu.roll(a, d, axis=1), a)
    h = h_sc[:, cs]
    out = []
    for gi in range(groups):
        hg = a[gi] * h + u[gi]
        out.append(hg)
        h = hg[7:8, :]
    h_sc[:, cs] = h
    return jnp.concatenate(out, axis=0) * gelu


REC_COLS = 2 * LRU_BLOCK


def _proj_kernel(normalize, tiles_per_batch, *refs):
    first = lax.rem(pl.program_id(0), tiles_per_batch) == 0
    if normalize:
        head_ref, x_ref, *refs = refs
    else:
        x_ref, *refs = refs
    w_ref, kvg_ref, lng_ref, lnb_ref, cw_ref, cb_ref, wa_ref, ba_ref, wx_ref, bx_ref, lam_ref, *refs = refs
    if normalize:
        h_ref, *refs = refs
        h = _layer_norm(jnp.where(first, head_ref[...], x_ref[...]), lng_ref[...], lnb_ref[...])
        h_ref[...] = h
        xb = h.astype(jnp.bfloat16)
    else:
        xb = x_ref[...].astype(jnp.bfloat16)
    hg_ref, q_ref, ckv_ref, qidx_ref, kidx_ref, widx_ref, ga_ref, gb_ref, xe_sc, h_sc = refs

    def cols(i, lo, width):
        return jnp.dot(xb, w_ref[:, P_OFFS[i] + lo:P_OFFS[i] + lo + width], preferred_element_type=jnp.float32)

    @pl.when(first)
    def _():
        xe_sc[0:8, :] = jnp.zeros((8, D_RNN), jnp.float32)
        h_sc[...] = jnp.zeros_like(h_sc)

    def normed_latents():
        c = cols(3, 0, P_CKV)
        ms = jnp.mean(c * c, axis=-1, keepdims=True)
        ckv_ref[...] = (c * lax.rsqrt(ms + LN_EPS) * kvg_ref[...]).astype(jnp.bfloat16)

    def index_keys():
        k = cols(5, 0, P_KIDX)
        lane = lax.broadcasted_iota(jnp.int32, k.shape, 1)
        mu = jnp.sum(k, axis=-1, keepdims=True) * (1.0 / IDX_DIM)
        d = jnp.where(lane < IDX_DIM, k - mu, 0.0)
        var = jnp.sum(d * d, axis=-1, keepdims=True) * (1.0 / IDX_DIM)
        kidx_ref[...] = (d * lax.rsqrt(var + LN_EPS)).astype(jnp.bfloat16)
        widx_ref[...] = cols(6, 0, P_WIDX) * (IDX_HEADS ** -0.5 * IDX_DIM ** -0.5)

    def plain(ref, i, lo, width, act=None):
        def run():
            y = cols(i, lo, width)
            ref[:, lo:lo + width] = (y if act is None else act(y)).astype(ref.dtype)
        return run

    half = D_MODEL // 2
    others = [plain(q_ref, 2, 0, half), plain(q_ref, 2, half, half), normed_latents, plain(qidx_ref, 4, 0, P_QIDX),
              index_keys, plain(ga_ref, 7, 0, half), plain(ga_ref, 7, half, half),
              plain(gb_ref, 8, 0, half), plain(gb_ref, 8, half, half)]
    n_slices = D_RNN // REC_COLS
    per_slice = -(-len(others) // n_slices)
    for s in range(n_slices):
        cs = slice(s * REC_COLS, (s + 1) * REC_COLS)
        g = cols(1, cs.start, REC_COLS)
        gelu = 0.5 * g * (1.0 + jnp.tanh(math.sqrt(2.0 / math.pi) * (g + 0.044715 * (g * g * g))))
        hg_ref[:, cs] = _recurrent_cols(cols(0, cs.start, REC_COLS), gelu, cs, first, cw_ref, cb_ref, wa_ref, ba_ref,
                                        wx_ref, bx_ref, lam_ref, xe_sc, h_sc).astype(jnp.bfloat16)
        for run in others[s * per_slice:(s + 1) * per_slice]:
            run()


def _proj(acts, w_in_p, kv_g, ln_g, ln_b, rec, tm, tiles_per_batch, n):
    normalize = len(acts) == 2
    f32, bf16 = jnp.float32, jnp.bfloat16
    widths = (D_RNN, P_Q, P_CKV, P_QIDX, P_KIDX, P_WIDX, P_GA, P_GB)
    dtypes = (bf16, bf16, bf16, bf16, bf16, f32, f32, f32)
    if normalize:
        widths, dtypes = (D_MODEL,) + widths, (f32,) + dtypes
        seq = acts[1].shape[0] // (n // (tiles_per_batch * tm))
        lead = tiles_per_batch * tm - seq

        def seq_offset(i):
            bi, j = i // tiles_per_batch, lax.rem(i, tiles_per_batch)
            return pl.multiple_of(bi * seq + jnp.maximum(j * tm - lead, 0), 8), 0

        act_specs = [pl.BlockSpec((tm, D_MODEL), lambda i: (i // tiles_per_batch, 0)),
                     pl.BlockSpec((pl.Element(tm), pl.Element(D_MODEL)), seq_offset)]
    else:
        act_specs = [pl.BlockSpec((tm, D_MODEL), lambda i: (i, 0))]
    return pl.pallas_call(
        functools.partial(_proj_kernel, normalize, tiles_per_batch),
        grid=(n // tm,),
        in_specs=act_specs + [
            _resident((D_MODEL, P_TOTAL)),
            _resident((1, KV_RANK)),
            _resident((1, D_MODEL)),
            _resident((1, D_MODEL)),
            _resident((CONV_WIDTH, D_RNN)),
            _resident((1, D_RNN)),
            _resident((LRU_BLOCKS, LRU_BLOCK, LRU_BLOCK)),
            _resident((1, D_RNN)),
            _resident((LRU_BLOCKS, LRU_BLOCK, LRU_BLOCK)),
            _resident((1, D_RNN)),
            _resident((1, D_RNN)),
        ],
        out_specs=[pl.BlockSpec((tm, w), lambda i: (i, 0)) for w in widths],
        out_shape=[jax.ShapeDtypeStruct((n, w), dt) for w, dt in zip(widths, dtypes)],
        scratch_shapes=[
            pltpu.VMEM((tm + 8, D_RNN), jnp.float32),
            pltpu.VMEM((1, D_RNN), jnp.float32),
        ],
        compiler_params=_cparams(("arbitrary",)),
        name="proj",
    )(*acts, w_in_p, kv_g, ln_g, ln_b, *rec)


N_COLS = N_HEADS * ROW_TILE
COL_TILE = 2 * ROW_TILE
LOWEST = -3.0e38


def _ordered_bits_to_f32(k):
    return pltpu.bitcast(k ^ (lax.shift_right_arithmetic(k, 31) & 0x7FFFFFFF), jnp.float32)


def _attn_kernel(k_top, n_blocks, qi_ref, wi_ref, q_ref, kidx_ref, ckv_ref, ckvt_ref, wuk_ref, wuv_ref, triu_ref,
                 o_ref, sc_sc, sc16_sc, qit_sc, qabst_sc, acct_sc, m_sc, l_sc, alpha_sc, p_sc, s0_sc, thr_sc,
                 excess_sc):
    t = pl.program_id(1)
    n_chunks = sc_sc.shape[1]
    chunk_blocks = KEY_CHUNK // ROW_TILE
    blk_s = jnp.minimum(t, n_blocks - 1)
    nch_s = blk_s // chunk_blocks + 1
    nch_a = jnp.where(t > 0, (t - 1) // chunk_blocks + 1, 0)
    slot_s = lax.rem(t, 2)
    slot_a = 1 - slot_s
    q_pos = blk_s * ROW_TILE + lax.broadcasted_iota(jnp.int32, (1, ROW_TILE), 1)
    k_iota = lax.broadcasted_iota(jnp.int32, (KEY_CHUNK, 1), 0)

    qit = qi_ref[0].astype(jnp.float32).T
    for h in range(IDX_HEADS):
        qit_sc[:, h * ROW_TILE:(h + 1) * ROW_TILE] = qit[h * IDX_DIM:(h + 1) * IDX_DIM, :].astype(jnp.bfloat16)
    wit = wi_ref[0].T
    wi_row = jnp.concatenate([wit[h:h + 1, :] for h in range(IDX_HEADS)], axis=1)
    qt = q_ref[0].astype(jnp.float32).T
    scale = HEAD_DIM ** -0.5 * math.log2(math.e)
    for h in range(N_HEADS):
        qa = jnp.dot(wuk_ref[h], qt[h * HEAD_DIM:(h + 1) * HEAD_DIM, :].astype(jnp.bfloat16),
                     preferred_element_type=jnp.float32)
        qabst_sc[:, h * ROW_TILE:(h + 1) * ROW_TILE] = (qa * scale).astype(jnp.bfloat16)

    def score_chunk(c):
        ks = kidx_ref[0, c][:, 0:IDX_DIM]
        sc = None
        for j in range(N_COLS // COL_TILE):
            cols = slice(j * COL_TILE, (j + 1) * COL_TILE)
            lg = jnp.dot(ks, qit_sc[:, cols], preferred_element_type=jnp.float32)
            w = jnp.maximum(lg, 0.0) * wi_row[:, cols]
            part = w[:, :ROW_TILE] + w[:, ROW_TILE:]
            sc = part if sc is None else sc + part
        k_pos = c * KEY_CHUNK + k_iota
        visible = jnp.logical_and(k_pos <= q_pos, k_pos >= PADF)
        sc = jnp.where(visible, sc, -jnp.inf)
        sc_sc[slot_s, c] = sc
        sc16_sc[slot_s, c] = sc.astype(jnp.bfloat16)

    @pl.when(t == 0)
    def _():
        thr_sc[...] = jnp.full_like(thr_sc, LOWEST)

    thr_a = thr_sc[...]
    n_tiles = N_COLS // COL_TILE
    last = slice((n_tiles - 1) * COL_TILE, n_tiles * COL_TILE)
    m_sc[...] = jnp.full_like(m_sc, NEG_INF)
    l_sc[...] = jnp.zeros_like(l_sc)
    acct_sc[...] = jnp.zeros_like(acct_sc)
    alpha_sc[:, last] = jnp.ones((1, COL_TILE), jnp.float32)
    p_sc[:, last] = jnp.zeros((KEY_CHUNK, COL_TILE), jnp.bfloat16)

    def logits_tile(kc, j):
        cols = slice(j * COL_TILE, (j + 1) * COL_TILE)
        return jnp.dot(kc, qabst_sc[:, cols], preferred_element_type=jnp.float32)

    def softmax_tile(s, sel, j):
        cols = slice(j * COL_TILE, (j + 1) * COL_TILE)
        s = jnp.concatenate([jnp.where(sel, s[:, :ROW_TILE], NEG_INF),
                             jnp.where(sel, s[:, ROW_TILE:], NEG_INF)], axis=1)
        m_prev = m_sc[:, cols]
        m_new = jnp.maximum(m_prev, jnp.max(s, axis=0, keepdims=True))
        p = jnp.exp2(s - m_new)
        alpha = jnp.exp2(m_prev - m_new)
        l_sc[:, cols] = alpha * l_sc[:, cols] + jnp.sum(p, axis=0, keepdims=True)
        m_sc[:, cols] = m_new
        alpha_sc[:, cols] = alpha
        p_sc[:, cols] = p.astype(jnp.bfloat16)

    def value_tile(kct, j):
        cols = slice(j * COL_TILE, (j + 1) * COL_TILE)
        acct_sc[:, cols] = alpha_sc[:, cols] * acct_sc[:, cols] + jnp.dot(
            kct, p_sc[:, cols], preferred_element_type=jnp.float32)

    def attn_chunk(c):
        kc = ckv_ref[0, c]
        kct = ckvt_ref[0, c]
        sel = sc_sc[slot_a, c] >= thr_a
        softmax_tile(s0_sc[...], sel, 0)
        value_tile(ckvt_ref[0, jnp.maximum(c - 1, 0)], n_tiles - 1)
        for j in range(1, n_tiles):
            softmax_tile(logits_tile(kc, j), sel, j)
            value_tile(kct, j - 1)
        s0_sc[...] = logits_tile(ckv_ref[0, jnp.minimum(c + 1, n_chunks - 1)], 0)

    def both(c, carry):
        attn_chunk(c)
        score_chunk(c)
        return carry

    def score_only(c, carry):
        score_chunk(c)
        return carry

    s0_sc[...] = logits_tile(ckv_ref[0, 0], 0)
    lax.fori_loop(0, nch_a, both, 0)
    lax.fori_loop(nch_a, nch_s, score_only, 0)

    @pl.when(t > 0)
    def _():
        value_tile(ckvt_ref[0, nch_a - 1], n_tiles - 1)
        o_lat_t = acct_sc[...] / l_sc[...]
        for h in range(N_HEADS):
            o_lat = o_lat_t[:, h * ROW_TILE:(h + 1) * ROW_TILE].T.astype(jnp.bfloat16)
            oh = jnp.dot(o_lat, wuv_ref[h], preferred_element_type=jnp.float32)
            o_ref[0, :, h * HEAD_DIM:(h + 1) * HEAD_DIM] = oh.astype(jnp.bfloat16)

    def threshold_search(n):
        def count_ge(cand):
            acc = jnp.zeros((8, ROW_TILE), jnp.int32)
            for c in range(n):
                hit = jnp.where(sc_sc[slot_s, c] >= cand, 1, 0)
                acc = acc + jnp.sum(hit.reshape(KEY_CHUNK // 8, 8, ROW_TILE), axis=0)
            return jnp.sum(acc, axis=0, keepdims=True)

        def count16_ge(cand):
            one = jnp.ones((), jnp.bfloat16)
            zero = jnp.zeros((), jnp.bfloat16)
            acc = jnp.zeros((16, ROW_TILE), jnp.float32)
            for c in range(n):
                hit = jnp.where(sc16_sc[slot_s, c] >= cand, one, zero).reshape(KEY_CHUNK // 16, 16, ROW_TILE)
                parts = [hit[i] for i in range(KEY_CHUNK // 16)]
                while len(parts) > 1:
                    parts = [parts[i] + parts[i + 1] for i in range(0, len(parts), 2)]
                acc = acc + parts[0].astype(jnp.float32)
            return jnp.sum(acc, axis=0, keepdims=True)

        def coarse_pass(i, base):
            cand = base + lax.shift_left(jnp.int32(1), 15 - i)
            raw = cand ^ (lax.shift_right_arithmetic(cand, 15) & 0x7FFF)
            cand_f = pltpu.bitcast(lax.shift_left(raw, 16), jnp.float32)
            cnt = count16_ge(cand_f.astype(jnp.bfloat16))
            return jnp.where(cnt >= k_top, cand, base)

        v16 = lax.fori_loop(0, 16, coarse_pass, jnp.full((1, ROW_TILE), -(2 ** 15), jnp.int32))
        few = v16 == -(2 ** 15)

        centre = lax.shift_left(v16, 16) + jnp.where(v16 < 0, 2 ** 16 - 1, 0)
        lo_k = centre - (2 ** 15 + 1)
        span = 3 * 2 ** 15 + 1

        def fine_pass(i, state):
            off, n_ge = state
            cand_off = off + lax.shift_left(jnp.int32(1), 16 - i)
            cnt = count_ge(_ordered_bits_to_f32(lo_k + cand_off))
            keep = jnp.logical_and(cnt >= k_top, cand_off < span)
            return jnp.where(keep, cand_off, off), jnp.where(keep, cnt, n_ge)

        off, n_ge = lax.fori_loop(0, 17, fine_pass, (jnp.zeros((1, ROW_TILE), jnp.int32),
                                                     jnp.zeros((1, ROW_TILE), jnp.int32)))
        thr_sc[...] = jnp.where(few, LOWEST, _ordered_bits_to_f32(lo_k + off))
        excess_sc[...] = jnp.where(few, 0, jnp.maximum(n_ge - k_top, 0))

    for n in range(1, n_chunks + 1):
        pl.when(jnp.logical_and(nch_s == n, t < n_blocks))(functools.partial(threshold_search, n))
    thr = thr_sc[...]
    excess = excess_sc[...]

    @pl.when(jnp.logical_and(jnp.max(excess) > 0, t < n_blocks))
    def _():
        surplus = excess.astype(jnp.float32)

        def demote(i, after):
            c = nch_s - 1 - i
            s = sc_sc[slot_s, c]
            tied = s == thr
            rank = after + jnp.dot(triu_ref[...], jnp.where(tied, 1.0, 0.0).astype(jnp.bfloat16),
                                   preferred_element_type=jnp.float32)
            sc_sc[slot_s, c] = jnp.where(jnp.logical_and(tied, rank <= surplus), -jnp.inf, s)
            return rank[0:1, :]

        lax.fori_loop(0, nch_s, demote, jnp.zeros((1, ROW_TILE), jnp.float32))


def _sparse_attention(qidx, widx, q, kidx_c, ckv_c, ckvt_c, wuk, wuv, k_top):
    B, tp, _ = q.shape
    nch = kidx_c.shape[1]
    nblk = tp // ROW_TILE

    def scored(bi, t):
        return (bi, jnp.minimum(t, nblk - 1), 0)

    def attended(bi, t):
        return (bi, jnp.maximum(t - 1, 0), 0)

    return pl.pallas_call(
        functools.partial(_attn_kernel, k_top, nblk),
        grid=(B, nblk + 1),
        in_specs=[
            pl.BlockSpec((1, ROW_TILE, P_QIDX), scored),
            pl.BlockSpec((1, ROW_TILE, P_WIDX), scored),
            pl.BlockSpec((1, ROW_TILE, P_Q), attended),
            pl.BlockSpec((1, nch, KEY_CHUNK, P_KIDX), lambda bi, t: (bi, 0, 0, 0)),
            pl.BlockSpec((1, nch, KEY_CHUNK, P_CKV), lambda bi, t: (bi, 0, 0, 0)),
            pl.BlockSpec((1, nch, P_CKV, KEY_CHUNK), lambda bi, t: (bi, 0, 0, 0)),
            _resident((N_HEADS, KV_RANK, HEAD_DIM)),
            _resident((N_HEADS, KV_RANK, HEAD_DIM)),
            _resident((KEY_CHUNK, KEY_CHUNK)),
        ],
        out_specs=pl.BlockSpec((1, ROW_TILE, P_Q), attended),
        out_shape=jax.ShapeDtypeStruct((B, tp, P_Q), jnp.bfloat16),
        scratch_shapes=[
            pltpu.VMEM((2, nch, KEY_CHUNK, ROW_TILE), jnp.float32),
            pltpu.VMEM((2, nch, KEY_CHUNK, ROW_TILE), jnp.bfloat16),
            pltpu.VMEM((IDX_DIM, N_COLS), jnp.bfloat16),
            pltpu.VMEM((KV_RANK, N_COLS), jnp.bfloat16),
            pltpu.VMEM((KV_RANK, N_COLS), jnp.float32),
            pltpu.VMEM((1, N_COLS), jnp.float32),
            pltpu.VMEM((1, N_COLS), jnp.float32),
            pltpu.VMEM((1, N_COLS), jnp.float32),
            pltpu.VMEM((KEY_CHUNK, N_COLS), jnp.bfloat16),
            pltpu.VMEM((KEY_CHUNK, COL_TILE), jnp.float32),
            pltpu.VMEM((1, ROW_TILE), jnp.float32),
            pltpu.VMEM((1, ROW_TILE), jnp.int32),
        ],
        compiler_params=_cparams(("parallel", "arbitrary")),
        name="sparse_attn",
    )(qidx, widx, q, kidx_c, ckv_c, ckvt_c, wuk, wuv, jnp.triu(jnp.ones((KEY_CHUNK, KEY_CHUNK), jnp.bfloat16)))


def _merge_kernel(alpha, x_ref, hg_ref, o_ref, ga_ref, gb_ref, wa_ref, wb_ref, wo_ref, g_ref, b_ref, y_ref):
    half = x_ref.shape[0] // 2
    for r in (slice(0, half), slice(half, 2 * half)):
        ya = jnp.dot(hg_ref[r, :], wa_ref[...], preferred_element_type=jnp.float32)
        yb = jnp.dot(o_ref[r, :], wb_ref[...], preferred_element_type=jnp.float32)
        mixed = _sigmoid(ga_ref[r, :]) * ya + _sigmoid(gb_ref[r, :]) * yb
        z = jnp.dot(mixed.astype(jnp.bfloat16), wo_ref[...], preferred_element_type=jnp.float32)
        y_ref[r, :] = _layer_norm(alpha * x_ref[r, :] + z, g_ref[...], b_ref[...])


def _merge(alpha, x2d, hg, o, ga, gb, w_a, w_b, w_o, g, b, tm):
    n = x2d.shape[0]

    def rows(w):
        return pl.BlockSpec((tm, w), lambda i: (i, 0))

    return pl.pallas_call(
        functools.partial(_merge_kernel, alpha),
        grid=(n // tm,),
        in_specs=[rows(D_MODEL), rows(D_RNN), rows(P_Q), rows(D_MODEL), rows(D_MODEL),
                  _resident((D_RNN, D_MODEL)), _resident((P_Q, D_MODEL)), _resident((D_MODEL, D_MODEL)),
                  _resident((1, D_MODEL)), _resident((1, D_MODEL))],
        out_specs=rows(D_MODEL),
        out_shape=jax.ShapeDtypeStruct((n, D_MODEL), jnp.float32),
        compiler_params=_cparams(("parallel",)),
        name="merge",
    )(x2d, hg, o, ga, gb, w_a, w_b, w_o, g, b)


def _mlp_kernel(alpha, x_ref, wu_ref, bu_ref, wd_ref, bd_ref, g_ref, b_ref, y_ref):
    x = x_ref[...]
    h = jnp.dot(x.astype(jnp.bfloat16), wu_ref[...], preferred_element_type=jnp.float32) + bu_ref[...]
    h = jnp.maximum(h, 0.0)
    h = (h * h).astype(jnp.bfloat16)
    z = jnp.dot(h, wd_ref[...], preferred_element_type=jnp.float32) + bd_ref[...]
    y_ref[...] = _layer_norm(alpha * x + z, g_ref[...], b_ref[...])


def _mlp(alpha, x2d, w_up, b_up, w_down, b_down, g, b, tm):
    n = x2d.shape[0]
    return pl.pallas_call(
        functools.partial(_mlp_kernel, alpha),
        grid=(n // tm,),
        in_specs=[pl.BlockSpec((tm, D_MODEL), lambda i: (i, 0)),
                  _resident((D_MODEL, D_FF)), _resident((1, D_FF)),
                  _resident((D_FF, D_MODEL)), _resident((1, D_MODEL)),
                  _resident((1, D_MODEL)), _resident((1, D_MODEL))],
        out_specs=pl.BlockSpec((tm, D_MODEL), lambda i: (i, 0)),
        out_shape=jax.ShapeDtypeStruct((n, D_MODEL), jnp.float32),
        compiler_params=_cparams(("parallel",)),
        name="mlp",
    )(x2d, w_up, b_up, w_down, b_down, g, b)


def _mlp_last(alpha, x2d, batch, rows, seq, w_up, b_up, w_down, b_down, g, b, tm):
    per_batch = seq // tm
    return pl.pallas_call(
        functools.partial(_mlp_kernel, alpha),
        grid=(batch, per_batch),
        in_specs=[pl.BlockSpec((pl.Element(tm), pl.Element(D_MODEL)),
                               lambda bi, i: (pl.multiple_of(bi * rows + ROW_TILE + i * tm, ROW_TILE), 0)),
                  _resident((D_MODEL, D_FF)), _resident((1, D_FF)),
                  _resident((D_FF, D_MODEL)), _resident((1, D_MODEL)),
                  _resident((1, D_MODEL)), _resident((1, D_MODEL))],
        out_specs=pl.BlockSpec((tm, D_MODEL), lambda bi, i: (bi * per_batch + i, 0)),
        out_shape=jax.ShapeDtypeStruct((batch * seq, D_MODEL), jnp.float32),
        compiler_params=_cparams(("parallel", "parallel")),
        name="mlp_last",
    )(x2d, w_up, b_up, w_down, b_down, g, b)


def _pad_in_proj(w_in):
    splits = (D_RNN, D_RNN, N_HEADS * HEAD_DIM, KV_RANK, IDX_HEADS * IDX_DIM, IDX_DIM, IDX_HEADS,
              D_MODEL, D_MODEL)
    padded = (P_LRU, P_G, P_Q, P_CKV, P_QIDX, P_KIDX, P_WIDX, P_GA, P_GB)
    parts, off = [], 0
    for w, pw in zip(splits, padded):
        part = w_in[:, off:off + w]
        if pw > w:
            part = jnp.pad(part, ((0, 0), (0, pw - w)))
        parts.append(part)
        off += w
    return jnp.concatenate(parts, axis=1).astype(jnp.bfloat16)


def kernel(x, meta_tokens, ln_in_g, ln_in_b, w_in, conv_w, conv_b, w_rg_a, b_rg_a, w_rg_x, b_rg_x, lru_lambda,
           kv_norm_g, w_uk, w_uv, w_branch_a, w_branch_b, w_out, ln1_g, ln1_b, w_up, b_up, w_down, b_down,
           ln2_g, ln2_b):
    B, S, _ = x.shape
    depth = w_in.shape[0]
    assert S % ROW_TILE == 0
    T = S + N_META
    k_top = min(TOPK_MAX, T // 4)
    alpha = (2.0 * depth) ** 0.25
    n_blk = S // ROW_TILE + 1
    tp = n_blk * ROW_TILE
    tk = -(-tp // KEY_CHUNK) * KEY_CHUNK
    n = B * tp
    tm_proj, tm = 320, 640
    assert tp % tm_proj == 0 and n % tm == 0
    bf16 = jnp.bfloat16

    def vec(a):
        return a.reshape(1, -1)

    meta_pad = jnp.pad(meta_tokens.astype(x.dtype), ((PADF, 0), (0, 0)))
    head = jnp.concatenate([jnp.broadcast_to(meta_pad[None], (B, ROW_TILE, D_MODEL)),
                            x[:, :tm_proj - ROW_TILE]], axis=1).reshape(B * tm_proj, D_MODEL)
    acts = (head, x.reshape(B * S, D_MODEL))

    for l in range(depth):
        rec = (conv_w[l], vec(conv_b[l]), w_rg_a[l].astype(bf16), vec(b_rg_a[l]), w_rg_x[l].astype(bf16),
               vec(b_rg_x[l]), vec(lru_lambda[l]))
        outs = _proj(acts, _pad_in_proj(w_in[l]), vec(kv_norm_g[l]), vec(ln_in_g), vec(ln_in_b), rec, tm_proj,
                     tp // tm_proj, n)
        if l == 0:
            h, *outs = outs
        hg, q, ckv, qidx, kidx, widx, ga, gb = outs

        key_pad = ((0, 0), (0, tk - tp), (0, 0))
        kidx_c = jnp.pad(kidx.reshape(B, tp, P_KIDX), key_pad).reshape(B, tk // KEY_CHUNK, KEY_CHUNK, P_KIDX)
        ckv_c = jnp.pad(ckv.reshape(B, tp, P_CKV), key_pad).reshape(B, tk // KEY_CHUNK, KEY_CHUNK, P_CKV)
        o = _sparse_attention(qidx.reshape(B, tp, P_QIDX), widx.reshape(B, tp, P_WIDX), q.reshape(B, tp, P_Q),
                              kidx_c, ckv_c, jnp.swapaxes(ckv_c, 2, 3),
                              w_uk[l].astype(bf16), w_uv[l].astype(bf16), k_top)

        h = _merge(alpha, h, hg, o.reshape(n, P_Q), ga, gb, w_branch_a[l].astype(bf16),
                   w_branch_b[l].astype(bf16), w_out[l].astype(bf16), vec(ln1_g[l]), vec(ln1_b[l]), tm)
        mlp_args = (w_up[l].astype(bf16), vec(b_up[l]), w_down[l].astype(bf16), vec(b_down[l]), vec(ln2_g[l]),
                    vec(ln2_b[l]))
        if l < depth - 1:
            h = _mlp(alpha, h, *mlp_args, tm)
            acts = (h,)
    return _mlp_last(alpha, h, B, tp, S, *mlp_args, 512).reshape(B, S, D_MODEL)
```

```python
import functools
import math

import jax
import jax.numpy as jnp
from jax import lax
from jax.experimental import pallas as pl
from jax.experimental.pallas import tpu as pltpu

D_MODEL = 1024
N_META = 16
D_RNN = 1280
LRU_BLOCKS = 10
LRU_BLOCK = D_RNN // LRU_BLOCKS
CONV_WIDTH = 4
LRU_C = 8.0
N_HEADS = 8
HEAD_DIM = 128
KV_RANK = 256
IDX_HEADS = 8
IDX_DIM = 64
TOPK_MAX = 256
D_FF = 4 * D_MODEL
LN_EPS = 1e-5
NEG_INF = -1e30

LANES = 128
ROW_TILE = 128
PADF = ROW_TILE - N_META
KEY_CHUNK = 512
VMEM_LIMIT = 56 * 1024 * 1024

P_LRU, P_G, P_Q, P_CKV, P_QIDX, P_KIDX, P_WIDX, P_GA, P_GB = (
    D_RNN, D_RNN, N_HEADS * HEAD_DIM, KV_RANK, IDX_HEADS * IDX_DIM, LANES, LANES, D_MODEL, D_MODEL)
P_OFFS = []
_acc = 0
for _w in (P_LRU, P_G, P_Q, P_CKV, P_QIDX, P_KIDX, P_WIDX, P_GA, P_GB):
    P_OFFS.append(_acc)
    _acc += _w
P_TOTAL = _acc


def _cparams(sem):
    return pltpu.CompilerParams(dimension_semantics=sem, vmem_limit_bytes=VMEM_LIMIT)


def _resident(shape):
    nd = len(shape)
    return pl.BlockSpec(shape, lambda *_: (0,) * nd, pipeline_mode=pl.Buffered(1))


def _layer_norm(x, g, b):
    mu = jnp.mean(x, axis=-1, keepdims=True)
    d = x - mu
    var = jnp.mean(d * d, axis=-1, keepdims=True)
    return d * lax.rsqrt(var + LN_EPS) * g + b


def _sigmoid(x):
    return 0.5 * jnp.tanh(0.5 * x) + 0.5


def _recurrent_cols(x, gelu, cs, first, cw_ref, cb_ref, wa_ref, ba_ref, wx_ref, bx_ref, lam_ref, xe_sc, h_sc):
    tm, width = x.shape
    row = lax.broadcasted_iota(jnp.int32, (tm, 1), 0)
    real = jnp.logical_or(jnp.logical_not(first), row >= PADF)
    x = jnp.where(real, x, 0.0)
    xe_sc[8:8 + tm, cs] = x
    cw = cw_ref[:, cs]
    xc = (cw[3:4] * x + cw[2:3] * xe_sc[7:7 + tm, cs] + cw[1:2] * xe_sc[6:6 + tm, cs]
          + cw[0:1] * xe_sc[5:5 + tm, cs] + cb_ref[:, cs])
    xe_sc[0:8, cs] = xe_sc[tm:tm + 8, cs]

    xcb = xc.astype(jnp.bfloat16)
    z = -lam_ref[:, cs]
    softplus = jnp.maximum(z, 0.0) + jnp.log1p(jnp.exp(-jnp.abs(z)))
    a_parts, u_parts = [], []
    for j in range(width // LRU_BLOCK):
        n = cs.start // LRU_BLOCK + j
        sl = slice(j * LRU_BLOCK, (j + 1) * LRU_BLOCK)
        gl = slice(cs.start + j * LRU_BLOCK, cs.start + (j + 1) * LRU_BLOCK)
        xn = xcb[:, sl]
        r = _sigmoid(jnp.dot(xn, wa_ref[n], preferred_element_type=jnp.float32) + ba_ref[:, gl])
        i = _sigmoid(jnp.dot(xn, wx_ref[n], preferred_element_type=jnp.float32) + bx_ref[:, gl])
        log_a = -LRU_C * r * softplus[:, sl]
        a = jnp.exp(log_a)
        u = jnp.sqrt(-jnp.tanh(log_a) * (a * a + 1.0)) * (i * xc[:, sl])
        a_parts.append(a)
        u_parts.append(jnp.where(real, u, 0.0))
    a = jnp.concatenate(a_parts, axis=1)
    u = jnp.concatenate(u_parts, axis=1)

    groups = tm // 8
    a = a.reshape(groups, 8, width)
    u = u.reshape(groups, 8, width)
    sub = lax.broadcasted_iota(jnp.int32, (1, 8, 1), 1)
    for d in (1, 2, 4):
        later = sub >= d
        u = jnp.where(later, a * pltpu.roll(u, d, axis=1) + u, u)
        a = jnp.where(later, a * pltpu.roll(a, d, axis=1), a)
    h = h_sc[:, cs]
    out = []
    for gi in range(groups):
        hg = a[gi] * h + u[gi]
        out.append(hg)
        h = hg[7:8, :]
    h_sc[:, cs] = h
    return jnp.concatenate(out, axis=0) * gelu


REC_COLS = 2 * LRU_BLOCK


def _proj_kernel(normalize, tiles_per_batch, *refs):
    first = lax.rem(pl.program_id(0), tiles_per_batch) == 0
    if normalize:
        head_ref, x_ref, *refs = refs
    else:
        x_ref, *refs = refs
    w_ref, kvg_ref, lng_ref, lnb_ref, cw_ref, cb_ref, wa_ref, ba_ref, wx_ref, bx_ref, lam_ref, *refs = refs
    if normalize:
        h_ref, *refs = refs
        h = _layer_norm(jnp.where(first, head_ref[...], x_ref[...]), lng_ref[...], lnb_ref[...])
        h_ref[...] = h
        xb = h.astype(jnp.bfloat16)
    else:
        xb = x_ref[...].astype(jnp.bfloat16)
    hg_ref, q_ref, ckv_ref, qidx_ref, kidx_ref, widx_ref, ga_ref, gb_ref, xe_sc, h_sc = refs

    def cols(i, lo, width):
        return jnp.dot(xb, w_ref[:, P_OFFS[i] + lo:P_OFFS[i] + lo + width], preferred_element_type=jnp.float32)

    @pl.when(first)
    def _():
        xe_sc[0:8, :] = jnp.zeros((8, D_RNN), jnp.float32)
        h_sc[...] = jnp.zeros_like(h_sc)

    def normed_latents():
        c = cols(3, 0, P_CKV)
        ms = jnp.mean(c * c, axis=-1, keepdims=True)
        ckv_ref[...] = (c * lax.rsqrt(ms + LN_EPS) * kvg_ref[...]).astype(jnp.bfloat16)

    def index_keys():
        k = cols(5, 0, P_KIDX)
        lane = lax.broadcasted_iota(jnp.int32, k.shape, 1)
        mu = jnp.sum(k, axis=-1, keepdims=True) * (1.0 / IDX_DIM)
        d = jnp.where(lane < IDX_DIM, k - mu, 0.0)
        var = jnp.sum(d * d, axis=-1, keepdims=True) * (1.0 / IDX_DIM)
        kidx_ref[...] = (d * lax.rsqrt(var + LN_EPS)).astype(jnp.bfloat16)
        widx_ref[...] = cols(6, 0, P_WIDX) * (IDX_HEADS ** -0.5 * IDX_DIM ** -0.5)

    def plain(ref, i, lo, width, act=None):
        def run():
            y = cols(i, lo, width)
            ref[:, lo:lo + width] = (y if act is None else act(y)).astype(ref.dtype)
        return run

    half = D_MODEL // 2
    others = [plain(q_ref, 2, 0, half), plain(q_ref, 2, half, half), normed_latents, plain(qidx_ref, 4, 0, P_QIDX),
              index_keys, plain(ga_ref, 7, 0, half, _sigmoid), plain(ga_ref, 7, half, half, _sigmoid),
              plain(gb_ref, 8, 0, half, _sigmoid), plain(gb_ref, 8, half, half, _sigmoid)]
    n_slices = D_RNN // REC_COLS
    per_slice = -(-len(others) // n_slices)
    for s in range(n_slices):
        cs = slice(s * REC_COLS, (s + 1) * REC_COLS)
        g = cols(1, cs.start, REC_COLS)
        gelu = 0.5 * g * (1.0 + jnp.tanh(math.sqrt(2.0 / math.pi) * (g + 0.044715 * (g * g * g))))
        hg_ref[:, cs] = _recurrent_cols(cols(0, cs.start, REC_COLS), gelu, cs, first, cw_ref, cb_ref, wa_ref, ba_ref,
                                        wx_ref, bx_ref, lam_ref, xe_sc, h_sc).astype(jnp.bfloat16)
        for run in others[s * per_slice:(s + 1) * per_slice]:
            run()


def _proj(acts, w_in_p, kv_g, ln_g, ln_b, rec, tm, tiles_per_batch, n):
    normalize = len(acts) == 2
    f32, bf16 = jnp.float32, jnp.bfloat16
    widths = (D_RNN, P_Q, P_CKV, P_QIDX, P_KIDX, P_WIDX, P_GA, P_GB)
    dtypes = (bf16, bf16, bf16, bf16, bf16, f32, f32, f32)
    if normalize:
        widths, dtypes = (D_MODEL,) + widths, (f32,) + dtypes
        seq = acts[1].shape[0] // (n // (tiles_per_batch * tm))
        lead = tiles_per_batch * tm - seq

        def seq_offset(i):
            bi, j = i // tiles_per_batch, lax.rem(i, tiles_per_batch)
            return pl.multiple_of(bi * seq + jnp.maximum(j * tm - lead, 0), 8), 0

        act_specs = [pl.BlockSpec((tm, D_MODEL), lambda i: (i // tiles_per_batch, 0)),
                     pl.BlockSpec((pl.Element(tm), pl.Element(D_MODEL)), seq_offset)]
    else:
        act_specs = [pl.BlockSpec((tm, D_MODEL), lambda i: (i, 0))]
    return pl.pallas_call(
        functools.partial(_proj_kernel, normalize, tiles_per_batch),
        grid=(n // tm,),
        in_specs=act_specs + [
            _resident((D_MODEL, P_TOTAL)),
            _resident((1, KV_RANK)),
            _resident((1, D_MODEL)),
            _resident((1, D_MODEL)),
            _resident((CONV_WIDTH, D_RNN)),
            _resident((1, D_RNN)),
            _resident((LRU_BLOCKS, LRU_BLOCK, LRU_BLOCK)),
            _resident((1, D_RNN)),
            _resident((LRU_BLOCKS, LRU_BLOCK, LRU_BLOCK)),
            _resident((1, D_RNN)),
            _resident((1, D_RNN)),
        ],
        out_specs=[pl.BlockSpec((tm, w), lambda i: (i, 0)) for w in widths],
        out_shape=[jax.ShapeDtypeStruct((n, w), dt) for w, dt in zip(widths, dtypes)],
        scratch_shapes=[
            pltpu.VMEM((tm + 8, D_RNN), jnp.float32),
            pltpu.VMEM((1, D_RNN), jnp.float32),
        ],
        compiler_params=_cparams(("arbitrary",)),
        name="proj",
    )(*acts, w_in_p, kv_g, ln_g, ln_b, *rec)


N_COLS = N_HEADS * ROW_TILE
COL_TILE = 2 * ROW_TILE
LOWEST = -3.0e38


def _ordered_bits_to_f32(k):
    return pltpu.bitcast(k ^ (lax.shift_right_arithmetic(k, 31) & 0x7FFFFFFF), jnp.float32)


def _attn_kernel(k_top, n_blocks, qi_ref, wi_ref, q_ref, kidx_ref, ckv_ref, ckvt_ref, wuk_ref, wuv_ref, triu_ref,
                 o_ref, sc_sc, sc16_sc, qit_sc, qabst_sc, acct_sc, m_sc, l_sc, alpha_sc, p_sc, s0_sc, thr_sc,
                 excess_sc):
    t = pl.program_id(1)
    n_chunks = sc_sc.shape[1]
    chunk_blocks = KEY_CHUNK // ROW_TILE
    blk_s = jnp.minimum(t, n_blocks - 1)
    nch_s = blk_s // chunk_blocks + 1
    nch_a = jnp.where(t > 0, (t - 1) // chunk_blocks + 1, 0)
    slot_s = lax.rem(t, 2)
    slot_a = 1 - slot_s
    q_pos = blk_s * ROW_TILE + lax.broadcasted_iota(jnp.int32, (1, ROW_TILE), 1)
    k_iota = lax.broadcasted_iota(jnp.int32, (KEY_CHUNK, 1), 0)

    qit = qi_ref[0].astype(jnp.float32).T
    for h in range(IDX_HEADS):
        qit_sc[:, h * ROW_TILE:(h + 1) * ROW_TILE] = qit[h * IDX_DIM:(h + 1) * IDX_DIM, :].astype(jnp.bfloat16)
    wit = wi_ref[0].T
    wi_row = jnp.concatenate([wit[h:h + 1, :] for h in range(IDX_HEADS)], axis=1)
    qt = q_ref[0].astype(jnp.float32).T
    scale = HEAD_DIM ** -0.5 * math.log2(math.e)
    for h in range(N_HEADS):
        qa = jnp.dot(wuk_ref[h], qt[h * HEAD_DIM:(h + 1) * HEAD_DIM, :].astype(jnp.bfloat16),
                     preferred_element_type=jnp.float32)
        qabst_sc[:, h * ROW_TILE:(h + 1) * ROW_TILE] = (qa * scale).astype(jnp.bfloat16)

    def score_chunk(c):
        ks = kidx_ref[0, c][:, 0:IDX_DIM]
        sc = None
        for j in range(N_COLS // COL_TILE):
            cols = slice(j * COL_TILE, (j + 1) * COL_TILE)
            lg = jnp.dot(ks, qit_sc[:, cols], preferred_element_type=jnp.float32)
            w = jnp.maximum(lg, 0.0) * wi_row[:, cols]
            part = w[:, :ROW_TILE] + w[:, ROW_TILE:]
            sc = part if sc is None else sc + part
        k_pos = c * KEY_CHUNK + k_iota
        visible = jnp.logical_and(k_pos <= q_pos, k_pos >= PADF)
        sc = jnp.where(visible, sc, -jnp.inf)
        sc_sc[slot_s, c] = sc
        sc16_sc[slot_s, c] = sc.astype(jnp.bfloat16)

    @pl.when(t == 0)
    def _():
        thr_sc[...] = jnp.full_like(thr_sc, LOWEST)

    thr_a = thr_sc[...]
    n_tiles = N_COLS // COL_TILE
    last = slice((n_tiles - 1) * COL_TILE, n_tiles * COL_TILE)
    m_sc[...] = jnp.full_like(m_sc, NEG_INF)
    l_sc[...] = jnp.zeros_like(l_sc)
    acct_sc[...] = jnp.zeros_like(acct_sc)
    alpha_sc[:, last] = jnp.ones((1, COL_TILE), jnp.float32)
    p_sc[:, last] = jnp.zeros((KEY_CHUNK, COL_TILE), jnp.bfloat16)

    def logits_tile(kc, j):
        cols = slice(j * COL_TILE, (j + 1) * COL_TILE)
        return jnp.dot(kc, qabst_sc[:, cols], preferred_element_type=jnp.float32)

    def softmax_tile(s, sel, j):
        cols = slice(j * COL_TILE, (j + 1) * COL_TILE)
        s = jnp.concatenate([jnp.where(sel, s[:, :ROW_TILE], NEG_INF),
                             jnp.where(sel, s[:, ROW_TILE:], NEG_INF)], axis=1)
        m_prev = m_sc[:, cols]
        m_new = jnp.maximum(m_prev, jnp.max(s, axis=0, keepdims=True))
        p = jnp.exp2(s - m_new)
        alpha = jnp.exp2(m_prev - m_new)
        l_sc[:, cols] = alpha * l_sc[:, cols] + jnp.sum(p, axis=0, keepdims=True)
        m_sc[:, cols] = m_new
        alpha_sc[:, cols] = alpha
        p_sc[:, cols] = p.astype(jnp.bfloat16)

    def value_tile(kct, j):
        cols = slice(j * COL_TILE, (j + 1) * COL_TILE)
        acct_sc[:, cols] = alpha_sc[:, cols] * acct_sc[:, cols] + jnp.dot(
            kct, p_sc[:, cols], preferred_element_type=jnp.float32)

    def attn_chunk(c):
        kc = ckv_ref[0, c]
        kct = ckvt_ref[0, c]
        sel = sc_sc[slot_a, c] >= thr_a
        softmax_tile(s0_sc[...], sel, 0)
        value_tile(ckvt_ref[0, jnp.maximum(c - 1, 0)], n_tiles - 1)
        for j in range(1, n_tiles):
            softmax_tile(logits_tile(kc, j), sel, j)
            value_tile(kct, j - 1)
        s0_sc[...] = logits_tile(ckv_ref[0, jnp.minimum(c + 1, n_chunks - 1)], 0)

    def both(c, carry):
        attn_chunk(c)
        score_chunk(c)
        return carry

    def score_only(c, carry):
        score_chunk(c)
        return carry

    s0_sc[...] = logits_tile(ckv_ref[0, 0], 0)
    lax.fori_loop(0, nch_a, both, 0)
    lax.fori_loop(nch_a, nch_s, score_only, 0)

    @pl.when(t > 0)
    def _():
        value_tile(ckvt_ref[0, nch_a - 1], n_tiles - 1)
        o_lat_t = acct_sc[...] / l_sc[...]
        for h in range(N_HEADS):
            o_lat = o_lat_t[:, h * ROW_TILE:(h + 1) * ROW_TILE].T.astype(jnp.bfloat16)
            oh = jnp.dot(o_lat, wuv_ref[h], preferred_element_type=jnp.float32)
            o_ref[0, :, h * HEAD_DIM:(h + 1) * HEAD_DIM] = oh.astype(jnp.bfloat16)

    def threshold_search(n):
        def count_ge(cand):
            acc = jnp.zeros((8, ROW_TILE), jnp.int32)
            for c in range(n):
                hit = jnp.where(sc_sc[slot_s, c] >= cand, 1, 0)
                acc = acc + jnp.sum(hit.reshape(KEY_CHUNK // 8, 8, ROW_TILE), axis=0)
            return jnp.sum(acc, axis=0, keepdims=True)

        def count16_ge(cand):
            one = jnp.ones((), jnp.bfloat16)
            zero = jnp.zeros((), jnp.bfloat16)
            acc = jnp.zeros((16, ROW_TILE), jnp.float32)
            for c in range(n):
                hit = jnp.where(sc16_sc[slot_s, c] >= cand, one, zero).reshape(KEY_CHUNK // 16, 16, ROW_TILE)
                parts = [hit[i] for i in range(KEY_CHUNK // 16)]
                while len(parts) > 1:
                    parts = [parts[i] + parts[i + 1] for i in range(0, len(parts), 2)]
                acc = acc + parts[0].astype(jnp.float32)
            return jnp.sum(acc, axis=0, keepdims=True)

        def coarse_pass(i, base):
            cand = base + lax.shift_left(jnp.int32(1), 15 - i)
            raw = cand ^ (lax.shift_right_arithmetic(cand, 15) & 0x7FFF)
            cand_f = pltpu.bitcast(lax.shift_left(raw, 16), jnp.float32)
            cnt = count16_ge(cand_f.astype(jnp.bfloat16))
            return jnp.where(cnt >= k_top, cand, base)

        v16 = lax.fori_loop(0, 16, coarse_pass, jnp.full((1, ROW_TILE), -(2 ** 15), jnp.int32))
        few = v16 == -(2 ** 15)

        centre = lax.shift_left(v16, 16) + jnp.where(v16 < 0, 2 ** 16 - 1, 0)
        lo_k = centre - (2 ** 15 + 1)
        span = 3 * 2 ** 15 + 1

        def fine_pass(i, state):
            off, n_ge = state
            cand_off = off + lax.shift_left(jnp.int32(1), 16 - i)
            cnt = count_ge(_ordered_bits_to_f32(lo_k + cand_off))
            keep = jnp.logical_and(cnt >= k_top, cand_off < span)
            return jnp.where(keep, cand_off, off), jnp.where(keep, cnt, n_ge)

        off, n_ge = lax.fori_loop(0, 17, fine_pass, (jnp.zeros((1, ROW_TILE), jnp.int32),
                                                     jnp.zeros((1, ROW_TILE), jnp.int32)))
        thr_sc[...] = jnp.where(few, LOWEST, _ordered_bits_to_f32(lo_k + off))
        excess_sc[...] = jnp.where(few, 0, jnp.maximum(n_ge - k_top, 0))

    for n in range(1, n_chunks + 1):
        pl.when(jnp.logical_and(nch_s == n, t < n_blocks))(functools.partial(threshold_search, n))
    thr = thr_sc[...]
    excess = excess_sc[...]

    @pl.when(jnp.logical_and(jnp.max(excess) > 0, t < n_blocks))
    def _():
        surplus = excess.astype(jnp.float32)

        def demote(i, after):
            c = nch_s - 1 - i
            s = sc_sc[slot_s, c]
            tied = s == thr
            rank = after + jnp.dot(triu_ref[...], jnp.where(tied, 1.0, 0.0).astype(jnp.bfloat16),
                                   preferred_element_type=jnp.float32)
            sc_sc[slot_s, c] = jnp.where(jnp.logical_and(tied, rank <= surplus), -jnp.inf, s)
            return rank[0:1, :]

        lax.fori_loop(0, nch_s, demote, jnp.zeros((1, ROW_TILE), jnp.float32))


def _sparse_attention(qidx, widx, q, kidx_c, ckv_c, ckvt_c, wuk, wuv, k_top):
    B, tp, _ = q.shape
    nch = kidx_c.shape[1]
    nblk = tp // ROW_TILE

    def scored(bi, t):
        return (bi, jnp.minimum(t, nblk - 1), 0)

    def attended(bi, t):
        return (bi, jnp.maximum(t - 1, 0), 0)

    return pl.pallas_call(
        functools.partial(_attn_kernel, k_top, nblk),
        grid=(B, nblk + 1),
        in_specs=[
            pl.BlockSpec((1, ROW_TILE, P_QIDX), scored),
            pl.BlockSpec((1, ROW_TILE, P_WIDX), scored),
            pl.BlockSpec((1, ROW_TILE, P_Q), attended),
            pl.BlockSpec((1, nch, KEY_CHUNK, P_KIDX), lambda bi, t: (bi, 0, 0, 0)),
            pl.BlockSpec((1, nch, KEY_CHUNK, P_CKV), lambda bi, t: (bi, 0, 0, 0)),
            pl.BlockSpec((1, nch, P_CKV, KEY_CHUNK), lambda bi, t: (bi, 0, 0, 0)),
            _resident((N_HEADS, KV_RANK, HEAD_DIM)),
            _resident((N_HEADS, KV_RANK, HEAD_DIM)),
            _resident((KEY_CHUNK, KEY_CHUNK)),
        ],
        out_specs=pl.BlockSpec((1, ROW_TILE, P_Q), attended),
        out_shape=jax.ShapeDtypeStruct((B, tp, P_Q), jnp.bfloat16),
        scratch_shapes=[
            pltpu.VMEM((2, nch, KEY_CHUNK, ROW_TILE), jnp.float32),
            pltpu.VMEM((2, nch, KEY_CHUNK, ROW_TILE), jnp.bfloat16),
            pltpu.VMEM((IDX_DIM, N_COLS), jnp.bfloat16),
            pltpu.VMEM((KV_RANK, N_COLS), jnp.bfloat16),
            pltpu.VMEM((KV_RANK, N_COLS), jnp.float32),
            pltpu.VMEM((1, N_COLS), jnp.float32),
            pltpu.VMEM((1, N_COLS), jnp.float32),
            pltpu.VMEM((1, N_COLS), jnp.float32),
            pltpu.VMEM((KEY_CHUNK, N_COLS), jnp.bfloat16),
            pltpu.VMEM((KEY_CHUNK, COL_TILE), jnp.float32),
            pltpu.VMEM((1, ROW_TILE), jnp.float32),
            pltpu.VMEM((1, ROW_TILE), jnp.int32),
        ],
        compiler_params=_cparams(("parallel", "arbitrary")),
        name="sparse_attn",
    )(qidx, widx, q, kidx_c, ckv_c, ckvt_c, wuk, wuv, jnp.triu(jnp.ones((KEY_CHUNK, KEY_CHUNK), jnp.bfloat16)))


def _merge_kernel(alpha, x_ref, hg_ref, o_ref, ga_ref, gb_ref, wa_ref, wb_ref, wo_ref, g_ref, b_ref, y_ref):
    half = x_ref.shape[0] // 2
    for r in (slice(0, half), slice(half, 2 * half)):
        ya = jnp.dot(hg_ref[r, :], wa_ref[...], preferred_element_type=jnp.float32)
        yb = jnp.dot(o_ref[r, :], wb_ref[...], preferred_element_type=jnp.float32)
        mixed = ga_ref[r, :] * ya + gb_ref[r, :] * yb
        z = jnp.dot(mixed.astype(jnp.bfloat16), wo_ref[...], preferred_element_type=jnp.float32)
        y_ref[r, :] = _layer_norm(alpha * x_ref[r, :] + z, g_ref[...], b_ref[...])


def _merge(alpha, x2d, hg, o, ga, gb, w_a, w_b, w_o, g, b, tm):
    n = x2d.shape[0]

    def rows(w):
        return pl.BlockSpec((tm, w), lambda i: (i, 0))

    return pl.pallas_call(
        functools.partial(_merge_kernel, alpha),
        grid=(n // tm,),
        in_specs=[rows(D_MODEL), rows(D_RNN), rows(P_Q), rows(D_MODEL), rows(D_MODEL),
                  _resident((D_RNN, D_MODEL)), _resident((P_Q, D_MODEL)), _resident((D_MODEL, D_MODEL)),
                  _resident((1, D_MODEL)), _resident((1, D_MODEL))],
        out_specs=rows(D_MODEL),
        out_shape=jax.ShapeDtypeStruct((n, D_MODEL), jnp.float32),
        compiler_params=_cparams(("parallel",)),
        name="merge",
    )(x2d, hg, o, ga, gb, w_a, w_b, w_o, g, b)


def _mlp_kernel(alpha, x_ref, wu_ref, bu_ref, wd_ref, bd_ref, g_ref, b_ref, y_ref):
    half = x_ref.shape[0] // 2
    for r in (slice(0, half), slice(half, 2 * half)):
        x = x_ref[r, :]
        h = jnp.dot(x.astype(jnp.bfloat16), wu_ref[...], preferred_element_type=jnp.float32) + bu_ref[...]
        h = jnp.maximum(h, 0.0)
        h = (h * h).astype(jnp.bfloat16)
        z = jnp.dot(h, wd_ref[...], preferred_element_type=jnp.float32) + bd_ref[...]
        y_ref[r, :] = _layer_norm(alpha * x + z, g_ref[...], b_ref[...])


def _mlp(alpha, x2d, w_up, b_up, w_down, b_down, g, b, tm):
    n = x2d.shape[0]
    return pl.pallas_call(
        functools.partial(_mlp_kernel, alpha),
        grid=(n // tm,),
        in_specs=[pl.BlockSpec((tm, D_MODEL), lambda i: (i, 0)),
                  _resident((D_MODEL, D_FF)), _resident((1, D_FF)),
                  _resident((D_FF, D_MODEL)), _resident((1, D_MODEL)),
                  _resident((1, D_MODEL)), _resident((1, D_MODEL))],
        out_specs=pl.BlockSpec((tm, D_MODEL), lambda i: (i, 0)),
        out_shape=jax.ShapeDtypeStruct((n, D_MODEL), jnp.float32),
        compiler_params=_cparams(("parallel",)),
        name="mlp",
    )(x2d, w_up, b_up, w_down, b_down, g, b)


def _mlp_last(alpha, x2d, batch, rows, seq, w_up, b_up, w_down, b_down, g, b, tm):
    per_batch = seq // tm
    return pl.pallas_call(
        functools.partial(_mlp_kernel, alpha),
        grid=(batch, per_batch),
        in_specs=[pl.BlockSpec((pl.Element(tm), pl.Element(D_MODEL)),
                               lambda bi, i: (pl.multiple_of(bi * rows + ROW_TILE + i * tm, ROW_TILE), 0)),
                  _resident((D_MODEL, D_FF)), _resident((1, D_FF)),
                  _resident((D_FF, D_MODEL)), _resident((1, D_MODEL)),
                  _resident((1, D_MODEL)), _resident((1, D_MODEL))],
        out_specs=pl.BlockSpec((tm, D_MODEL), lambda bi, i: (bi * per_batch + i, 0)),
        out_shape=jax.ShapeDtypeStruct((batch * seq, D_MODEL), jnp.float32),
        compiler_params=_cparams(("parallel", "parallel")),
        name="mlp_last",
    )(x2d, w_up, b_up, w_down, b_down, g, b)


def _pad_in_proj(w_in):
    splits = (D_RNN, D_RNN, N_HEADS * HEAD_DIM, KV_RANK, IDX_HEADS * IDX_DIM, IDX_DIM, IDX_HEADS,
              D_MODEL, D_MODEL)
    padded = (P_LRU, P_G, P_Q, P_CKV, P_QIDX, P_KIDX, P_WIDX, P_GA, P_GB)
    parts, off = [], 0
    for w, pw in zip(splits, padded):
        part = w_in[:, off:off + w]
        if pw > w:
            part = jnp.pad(part, ((0, 0), (0, pw - w)))
        parts.append(part)
        off += w
    return jnp.concatenate(parts, axis=1).astype(jnp.bfloat16)


def kernel(x, meta_tokens, ln_in_g, ln_in_b, w_in, conv_w, conv_b, w_rg_a, b_rg_a, w_rg_x, b_rg_x, lru_lambda,
           kv_norm_g, w_uk, w_uv, w_branch_a, w_branch_b, w_out, ln1_g, ln1_b, w_up, b_up, w_down, b_down,
           ln2_g, ln2_b):
    B, S, _ = x.shape
    depth = w_in.shape[0]
    assert S % ROW_TILE == 0
    T = S + N_META
    k_top = min(TOPK_MAX, T // 4)
    alpha = (2.0 * depth) ** 0.25
    n_blk = S // ROW_TILE + 1
    tp = n_blk * ROW_TILE
    tk = -(-tp // KEY_CHUNK) * KEY_CHUNK
    n = B * tp
    tm_proj, tm = 320, 640
    assert tp % tm_proj == 0 and n % tm == 0
    bf16 = jnp.bfloat16

    def vec(a):
        return a.reshape(1, -1)

    meta_pad = jnp.pad(meta_tokens.astype(x.dtype), ((PADF, 0), (0, 0)))
    head = jnp.concatenate([jnp.broadcast_to(meta_pad[None], (B, ROW_TILE, D_MODEL)),
                            x[:, :tm_proj - ROW_TILE]], axis=1).reshape(B * tm_proj, D_MODEL)
    acts = (head, x.reshape(B * S, D_MODEL))

    for l in range(depth):
        rec = (conv_w[l], vec(conv_b[l]), w_rg_a[l].astype(bf16), vec(b_rg_a[l]), w_rg_x[l].astype(bf16),
               vec(b_rg_x[l]), vec(lru_lambda[l]))
        outs = _proj(acts, _pad_in_proj(w_in[l]), vec(kv_norm_g[l]), vec(ln_in_g), vec(ln_in_b), rec, tm_proj,
                     tp // tm_proj, n)
        if l == 0:
            h, *outs = outs
        hg, q, ckv, qidx, kidx, widx, ga, gb = outs

        key_pad = ((0, 0), (0, tk - tp), (0, 0))
        kidx_c = jnp.pad(kidx.reshape(B, tp, P_KIDX), key_pad).reshape(B, tk // KEY_CHUNK, KEY_CHUNK, P_KIDX)
        ckv_c = jnp.pad(ckv.reshape(B, tp, P_CKV), key_pad).reshape(B, tk // KEY_CHUNK, KEY_CHUNK, P_CKV)
        o = _sparse_attention(qidx.reshape(B, tp, P_QIDX), widx.reshape(B, tp, P_WIDX), q.reshape(B, tp, P_Q),
                              kidx_c, ckv_c, jnp.swapaxes(ckv_c, 2, 3),
                              w_uk[l].astype(bf16), w_uv[l].astype(bf16), k_top)

        h = _merge(alpha, h, hg, o.reshape(n, P_Q), ga, gb, w_branch_a[l].astype(bf16),
                   w_branch_b[l].astype(bf16), w_out[l].astype(bf16), vec(ln1_g[l]), vec(ln1_b[l]), tm)
        mlp_args = (w_up[l].astype(bf16), vec(b_up[l]), w_down[l].astype(bf16), vec(b_down[l]), vec(ln2_g[l]),
                    vec(ln2_b[l]))
        if l < depth - 1:
            h = _mlp(alpha, h, *mlp_args, tm)
            acts = (h,)
    return _mlp_last(alpha, h, B, tp, S, *mlp_args, 512).reshape(B, S, D_MODEL)
```
